```python
import math
import jax, jax.numpy as jnp
from jax import lax
import numpy as np

D_MODEL = 1024
BATCH = 8
SEQ = 8192
DEPTH = 1
DEC_BATCH = 128
DEC_SEQ = 1
PAST_LEN = 8192
PAGE_SIZE = 128

SSM_EXPAND = 2
D_INNER = SSM_EXPAND * D_MODEL
SSM_HEAD_DIM = 64
N_SSM_HEADS = D_INNER // SSM_HEAD_DIM
N_SSM_GROUPS = 4
D_STATE = 128
CONV_W = 4
CONV_DIM = D_INNER + 2 * N_SSM_GROUPS * D_STATE
SSD_CHUNK = 128
ATTN_HEAD_DIM = 64
HEADS_PER_GROUP = 4
WINDOWS = (128, 512, 2048)
DILATIONS = (1, 4, 16)
N_ATTN_GROUPS = 3
N_ATTN_HEADS = N_ATTN_GROUPS * HEADS_PER_GROUP
ATTN_GROUP_WIDTH = HEADS_PER_GROUP * ATTN_HEAD_DIM
ATTN_QKV = N_ATTN_GROUPS * ATTN_GROUP_WIDTH
ATTN_BLOCK = 128
ALIBI_MAX_BIAS = 8.0
D_FF = -(-8 * D_MODEL // (3 * 256)) * 256
RMS_EPS = 1e-6
OFF_XBC = D_INNER
OFF_DT = OFF_XBC + CONV_DIM
OFF_Q = OFF_DT + N_SSM_HEADS
OFF_K = OFF_Q + ATTN_QKV
OFF_V = OFF_K + ATTN_QKV
OFF_GS = OFF_V + ATTN_QKV
OFF_GA = OFF_GS + D_MODEL
D_IN_PROJ = OFF_GA + D_MODEL
IN_SPLITS = (OFF_XBC, OFF_DT, OFF_Q, OFF_K, OFF_V, OFF_GS, OFF_GA)

kernel_name = 'ssd_dilated_attn_hybrid_step'


def rmsnorm(x, g):
    xf = x.astype(jnp.float32)
    y = xf * lax.rsqrt(jnp.mean(xf * xf, axis=-1, keepdims=True) + RMS_EPS)
    return (y * g.astype(jnp.float32)).astype(x.dtype)


def alibi_slopes():
    return jnp.exp2(-ALIBI_MAX_BIAS * jnp.arange(1, N_ATTN_HEADS + 1, dtype=jnp.float32) / N_ATTN_HEADS)


def ssd_scan(x, dt, a, b_mat, c_mat, s0):
    n, L, H, P = x.shape
    f32 = jnp.float32
    q = min(SSD_CHUNK, L)
    pad = (-L) % q
    xdt = x.astype(f32) * dt[..., None]
    adt = dt * a
    bm = b_mat.astype(f32)
    cm = c_mat.astype(f32)
    if pad:
        xdt = jnp.pad(xdt, ((0, 0), (0, pad), (0, 0), (0, 0)))
        adt = jnp.pad(adt, ((0, 0), (0, pad), (0, 0)))
        bm = jnp.pad(bm, ((0, 0), (0, pad), (0, 0), (0, 0)))
        cm = jnp.pad(cm, ((0, 0), (0, pad), (0, 0), (0, 0)))
    nc = (L + pad) // q
    r = H // N_SSM_GROUPS
    xdt = xdt.reshape(n, nc, q, N_SSM_GROUPS, r, P)
    adt = adt.reshape(n, nc, q, N_SSM_GROUPS, r)
    bm = bm.reshape(n, nc, q, N_SSM_GROUPS, D_STATE)
    cm = cm.reshape(n, nc, q, N_SSM_GROUPS, D_STATE)
    acum = jnp.cumsum(adt, axis=2)
    causal = jnp.tril(jnp.ones((q, q), dtype=bool))[:, :, None, None]
    decay = jnp.exp(jnp.where(causal, acum[:, :, :, None] - acum[:, :, None], -jnp.inf))
    cb = jnp.einsum('nctgd,ncsgd->nctsg', cm, bm)
    y_diag = jnp.einsum('nctsgr,ncsgrp->nctgrp', cb[..., None] * decay, xdt)
    decay_to_end = jnp.exp(acum[:, :, -1:] - acum)
    chunk_states = jnp.einsum('ncsgd,ncsgr,ncsgrp->ncgrpd', bm, decay_to_end, xdt)
    chunk_decay = jnp.exp(acum[:, :, -1])

    def carry_step(s, inp):
        st, dec = inp
        return s * dec[..., None, None] + st, s

    s_init = s0.astype(f32).reshape(n, N_SSM_GROUPS, r, P, D_STATE)
    s_last, s_enter = lax.scan(carry_step, s_init,
                               (jnp.moveaxis(chunk_states, 1, 0), jnp.moveaxis(chunk_decay, 1, 0)))
    s_enter = jnp.moveaxis(s_enter, 0, 1)
    y_off = jnp.einsum('nctgd,ncgrpd,nctgr->nctgrp', cm, s_enter, jnp.exp(acum))
    y = (y_diag + y_off).reshape(n, nc * q, H, P)[:, :L]
    return y.astype(x.dtype), s_last.reshape(n, H, P, D_STATE).astype(s0.dtype)


def mamba2_branch(z, xbc, dt_raw, conv_hist, ssm0, conv_w, conv_b, dt_bias, a_log, d_skip, ssm_norm_g):
    n, L, _ = z.shape
    xpad = jnp.concatenate([conv_hist.astype(xbc.dtype), xbc], axis=1)
    new_conv = xpad[:, xpad.shape[1] - (CONV_W - 1):]
    xc = lax.conv_general_dilated(xpad, conv_w[:, None, :].astype(xpad.dtype), (1,), 'VALID',
                                  dimension_numbers=('NWC', 'WIO', 'NWC'),
                                  feature_group_count=CONV_DIM) + conv_b
    xc = jax.nn.silu(xc)
    xs, bm, cm = jnp.split(xc, (D_INNER, D_INNER + N_SSM_GROUPS * D_STATE), axis=-1)
    xs = xs.reshape(n, L, N_SSM_HEADS, SSM_HEAD_DIM)
    bm = bm.reshape(n, L, N_SSM_GROUPS, D_STATE)
    cm = cm.reshape(n, L, N_SSM_GROUPS, D_STATE)
    dt = jax.nn.softplus(dt_raw.astype(jnp.float32) + dt_bias.astype(jnp.float32))
    a = -jnp.exp(a_log.astype(jnp.float32))
    y, s_new = ssd_scan(xs, dt, a, bm, cm, ssm0)
    y = y + d_skip[:, None].astype(y.dtype) * xs
    y = y.reshape(n, L, D_INNER) * jax.nn.silu(z)
    y = rmsnorm(y.reshape(n, L, N_SSM_GROUPS, D_INNER // N_SSM_GROUPS),
                ssm_norm_g.reshape(N_SSM_GROUPS, D_INNER // N_SSM_GROUPS)).reshape(n, L, D_INNER)
    return y, new_conv, s_new


def dilated_attn_prompt(q, k, v, window, dil, slopes):
    n, S, H, E = q.shape
    band = window // dil
    ls = S // dil
    nb = -(-ls // ATTN_BLOCK)
    pad = nb * ATTN_BLOCK - ls

    def to_blocks(t):
        t = t.reshape(n, ls, dil, H, E).transpose(0, 2, 1, 3, 4)
        t = jnp.pad(t, ((0, 0), (0, 0), (0, pad), (0, 0), (0, 0)))
        return t.reshape(n, dil, nb, ATTN_BLOCK, H, E)

    def with_prev(t):
        prev = jnp.pad(t, ((0, 0), (0, 0), (1, 0), (0, 0), (0, 0), (0, 0)))[:, :, :-1]
        return jnp.concatenate([prev, t], axis=3)

    qb = to_blocks(q)
    kc = with_prev(to_blocks(k))
    vc = with_prev(to_blocks(v))
    s = jnp.einsum('nrbqhe,nrbkhe->nrbhqk', qb, kc).astype(jnp.float32) * (ATTN_HEAD_DIM ** -0.5)
    qi = jnp.arange(ATTN_BLOCK)[:, None] + ATTN_BLOCK
    kj = jnp.arange(2 * ATTN_BLOCK)[None, :]
    delta = qi - kj
    blk = jnp.arange(nb)[:, None, None]
    valid = (delta >= 0) & (delta <= band) & ((blk > 0) | (kj >= ATTN_BLOCK))
    s = s - slopes[:, None, None] * (dil * delta).astype(jnp.float32)
    s = jnp.where(valid[:, None], s, -jnp.inf)
    m = jnp.max(s, axis=-1, keepdims=True)
    p = jnp.exp(s - m)
    den = jnp.sum(p, axis=-1)
    o = jnp.einsum('nrbhqk,nrbkhe->nrbqhe', p, vc.astype(jnp.float32))
    o = o / jnp.transpose(den, (0, 1, 2, 4, 3))[..., None]
    lse = jnp.transpose(m[..., 0] + jnp.log(den), (0, 1, 2, 4, 3))
    o = o.reshape(n, dil, nb * ATTN_BLOCK, H, E)[:, :, :ls].transpose(0, 2, 1, 3, 4).reshape(n, S, H, E)
    lse = lse.reshape(n, dil, nb * ATTN_BLOCK, H)[:, :, :ls].transpose(0, 2, 1, 3).reshape(n, S, H)
    return o, lse


def dilated_attn_sample(q, k_new, v_new, cache_kv, window, dil, slopes):
    n, T, H, E = q.shape
    lc = cache_kv.shape[1]
    keys = jnp.concatenate([cache_kv[:, :, 0].astype(k_new.dtype), k_new], axis=1)
    vals = jnp.concatenate([cache_kv[:, :, 1].astype(v_new.dtype), v_new], axis=1)
    n_keys = window // dil + 1
    off = jnp.arange(n_keys) * dil
    idx = lc + jnp.arange(T)[:, None] - off[None, :]
    valid = idx >= 0
    idx = jnp.maximum(idx, 0)
    kg = keys[:, idx]
    vg = vals[:, idx]
    s = jnp.einsum('nthe,ntkhe->nhtk', q, kg).astype(jnp.float32) * (ATTN_HEAD_DIM ** -0.5)
    s = s - slopes[:, None, None] * off.astype(jnp.float32)[None, None, :]
    s = jnp.where(valid[None, None], s, -jnp.inf)
    m = jnp.max(s, axis=-1, keepdims=True)
    p = jnp.exp(s - m)
    den = jnp.sum(p, axis=-1)
    o = jnp.einsum('nhtk,ntkhe->nthe', p, vg.astype(jnp.float32))
    o = o / jnp.transpose(den, (0, 2, 1))[..., None]
    lse = jnp.transpose(m[..., 0] + jnp.log(den), (0, 2, 1))
    return o, lse


def decoder_layer(x, conv_hist, ssm0, kv_caches, lp):
    n, L, _ = x.shape
    h = rmsnorm(x, lp['norm1_g'])
    proj = h @ lp['w_in']
    z, xbc, dt_raw, q, k, v, gate_s, gate_a = jnp.split(proj, IN_SPLITS, axis=-1)
    y_ssm, new_conv, new_ssm = mamba2_branch(z, xbc, dt_raw, conv_hist, ssm0, lp['conv_w'], lp['conv_b'],
                                             lp['dt_bias'], lp['a_log'], lp['d_skip'], lp['ssm_norm_g'])
    slopes = alibi_slopes()
    outs, lses, new_kv = [], [], []
    shp = (n, L, HEADS_PER_GROUP, ATTN_HEAD_DIM)
    for g in range(N_ATTN_GROUPS):
        cols = slice(g * ATTN_GROUP_WIDTH, (g + 1) * ATTN_GROUP_WIDTH)
        qg = q[..., cols].reshape(shp)
        kg = k[..., cols].reshape(shp)
        vg = v[..., cols].reshape(shp)
        sg = slopes[g * HEADS_PER_GROUP:(g + 1) * HEADS_PER_GROUP]
        if kv_caches is None:
            o, lse = dilated_attn_prompt(qg, kg, vg, WINDOWS[g], DILATIONS[g], sg)
            keep = min(WINDOWS[g], L)
            new_kv.append(jnp.stack([kg[:, L - keep:], vg[:, L - keep:]], axis=2))
        else:
            o, lse = dilated_attn_sample(qg, kg, vg, kv_caches[g], WINDOWS[g], DILATIONS[g], sg)
            new_kv.append(jnp.stack([kg, vg], axis=2))
        outs.append(o)
        lses.append(lse)
    alpha = jax.nn.softmax(jnp.stack(lses), axis=0)
    o_attn = jnp.einsum('gnlh,gnlhe->nlhe', alpha, jnp.stack(outs)).astype(x.dtype).reshape(n, L, ATTN_GROUP_WIDTH)
    merged = (jax.nn.sigmoid(gate_s) * (y_ssm @ lp['w_ssm_out'])
              + jax.nn.sigmoid(gate_a) * (o_attn @ lp['w_attn_out']))
    x = x + merged @ lp['w_o']
    h2 = rmsnorm(x, lp['norm2_g'])
    x = x + (jax.nn.silu(h2 @ lp['w_ffn_gate']) * (h2 @ lp['w_ffn_up'])) @ lp['w_ffn_down']
    return x, new_conv, new_ssm, new_kv


def setup_inputs(seed: int = 0) -> dict:
    key = jax.random.key(seed)
    ks = jax.random.split(key, 26)
    f32 = jnp.float32

    def nrm(k, shape, scale):
        return jax.random.normal(k, shape, f32) * scale

    win = [min(w, PAST_LEN) for w in WINDOWS]
    kv_tail = (2, HEADS_PER_GROUP, ATTN_HEAD_DIM)
    dt0 = jnp.exp(jax.random.uniform(ks[10], (DEPTH, N_SSM_HEADS), f32, math.log(1e-3), math.log(1e-1)))
    return {
        'x_prompt': nrm(ks[0], (BATCH, SEQ, D_MODEL), 1.0),
        'x_sample': nrm(ks[1], (DEC_BATCH, DEC_SEQ, D_MODEL), 1.0),
        'state_ssm': nrm(ks[2], (DEPTH, DEC_BATCH, N_SSM_HEADS, SSM_HEAD_DIM, D_STATE), 0.5),
        'state_conv': nrm(ks[3], (DEPTH, DEC_BATCH, CONV_W - 1, CONV_DIM), 1.0),
        'cache_kv_w128': nrm(ks[4], (DEPTH, DEC_BATCH, win[0]) + kv_tail, 1.0),
        'cache_kv_w512': nrm(ks[5], (DEPTH, DEC_BATCH, win[1]) + kv_tail, 1.0),
        'cache_kv_w2048': nrm(ks[6], (DEPTH, DEC_BATCH, win[2]) + kv_tail, 1.0),
        'norm1_g': 1.0 + nrm(ks[7], (DEPTH, D_MODEL), 0.01),
        'w_in': nrm(ks[8], (DEPTH, D_MODEL, D_IN_PROJ), D_MODEL ** -0.5),
        'conv_w': nrm(ks[9], (DEPTH, CONV_W, CONV_DIM), CONV_W ** -0.5),
        'conv_b': nrm(ks[11], (DEPTH, CONV_DIM), 0.01),
        'dt_bias': dt0 + jnp.log(-jnp.expm1(-dt0)),
        'a_log': jnp.log(jax.random.uniform(ks[12], (DEPTH, N_SSM_HEADS), f32, 1.0, 16.0)),
        'd_skip': 1.0 + nrm(ks[13], (DEPTH, N_SSM_HEADS), 0.1),
        'ssm_norm_g': 1.0 + nrm(ks[14], (DEPTH, D_INNER), 0.01),
        'w_ssm_out': nrm(ks[15], (DEPTH, D_INNER, D_MODEL), D_INNER ** -0.5),
        'w_attn_out': nrm(ks[16], (DEPTH, ATTN_GROUP_WIDTH, D_MODEL), ATTN_GROUP_WIDTH ** -0.5),
        'w_o': nrm(ks[17], (DEPTH, D_MODEL, D_MODEL), D_MODEL ** -0.5),
        'norm2_g': 1.0 + nrm(ks[18], (DEPTH, D_MODEL), 0.01),
        'w_ffn_gate': nrm(ks[19], (DEPTH, D_MODEL, D_FF), D_MODEL ** -0.5),
        'w_ffn_up': nrm(ks[20], (DEPTH, D_MODEL, D_FF), D_MODEL ** -0.5),
        'w_ffn_down': nrm(ks[21], (DEPTH, D_FF, D_MODEL), D_FF ** -0.5),
        'norm_f_g': 1.0 + nrm(ks[22], (D_MODEL,), 0.01),
    }


def reference(x_prompt, x_sample, state_ssm, state_conv, cache_kv_w128, cache_kv_w512, cache_kv_w2048,
              norm1_g, w_in, conv_w, conv_b, dt_bias, a_log, d_skip, ssm_norm_g, w_ssm_out, w_attn_out,
              w_o, norm2_g, w_ffn_gate, w_ffn_up, w_ffn_down, norm_f_g):
    xp, xs = x_prompt, x_sample
    ssm_p, ssm_s, conv_p, conv_s = [], [], [], []
    kv_p = [[], [], []]
    kv_s = [[], [], []]
    for l in range(DEPTH):
        lp = {'norm1_g': norm1_g[l], 'w_in': w_in[l], 'conv_w': conv_w[l], 'conv_b': conv_b[l],
              'dt_bias': dt_bias[l], 'a_log': a_log[l], 'd_skip': d_skip[l], 'ssm_norm_g': ssm_norm_g[l],
              'w_ssm_out': w_ssm_out[l], 'w_attn_out': w_attn_out[l], 'w_o': w_o[l], 'norm2_g': norm2_g[l],
              'w_ffn_gate': w_ffn_gate[l], 'w_ffn_up': w_ffn_up[l], 'w_ffn_down': w_ffn_down[l]}
        conv0 = jnp.zeros((xp.shape[0], CONV_W - 1, CONV_DIM), xp.dtype)
        ssm0 = jnp.zeros((xp.shape[0], N_SSM_HEADS, SSM_HEAD_DIM, D_STATE), jnp.float32)
        xp, c_new, s_new, kv_new = decoder_layer(xp, conv0, ssm0, None, lp)
        conv_p.append(c_new)
        ssm_p.append(s_new)
        for g in range(N_ATTN_GROUPS):
            kv_p[g].append(kv_new[g])
        xs, c_new, s_new, kv_new = decoder_layer(
            xs, state_conv[l], state_ssm[l], (cache_kv_w128[l], cache_kv_w512[l], cache_kv_w2048[l]), lp)
        conv_s.append(c_new)
        ssm_s.append(s_new)
        for g in range(N_ATTN_GROUPS):
            kv_s[g].append(kv_new[g])
    y_prompt = rmsnorm(xp, norm_f_g)
    y_sample = rmsnorm(xs, norm_f_g)
    new_ssm_prompt = jnp.stack(ssm_p)
    new_ssm_sample = jnp.stack(ssm_s)
    new_conv_prompt = jnp.stack(conv_p)
    new_conv_sample = jnp.stack(conv_s)
    new_kv128_prompt = jnp.stack(kv_p[0])
    new_kv128_sample = jnp.stack(kv_s[0])
    new_kv512_prompt = jnp.stack(kv_p[1])
    new_kv512_sample = jnp.stack(kv_s[1])
    new_kv2048_prompt = jnp.stack(kv_p[2])
    new_kv2048_sample = jnp.stack(kv_s[2])
    return (y_prompt, y_sample, new_ssm_prompt, new_ssm_sample, new_conv_prompt, new_conv_sample,
            new_kv128_prompt, new_kv128_sample, new_kv512_prompt, new_kv512_sample,
            new_kv2048_prompt, new_kv2048_sample)
```

```python
import functools

import jax
import jax.numpy as jnp
from jax import lax
from jax.experimental import pallas as pl
from jax.experimental.pallas import tpu as pltpu

F32 = jnp.float32
BF16 = jnp.bfloat16

D_MODEL = 1024
D_INNER = 2048
HEAD_DIM = 64
N_HEADS = 32
N_GROUPS = 4
HEADS_PER_SSM_GROUP = N_HEADS // N_GROUPS
D_STATE = 128
CONV_W = 4
CONV_DIM = D_INNER + 2 * N_GROUPS * D_STATE
CHUNK = 128
DT_PAD = 128
ATTN_HEADS = 4
ATTN_DIM = 64
ATTN_WIDTH = ATTN_HEADS * ATTN_DIM
WINDOWS = (128, 512, 2048)
DILATIONS = (1, 4, 16)
ATTN_BLOCK = 128
D_FF = 2816
RMS_EPS = 1e-6
NEG_BIG = -1e30
OFF_XBC = D_INNER
OFF_DT = OFF_XBC + CONV_DIM
OFF_Q = OFF_DT + N_HEADS
OFF_K = OFF_Q + 3 * ATTN_WIDTH
OFF_V = OFF_K + 3 * ATTN_WIDTH
OFF_GS = OFF_V + 3 * ATTN_WIDTH
W_SSD_COLS = D_INNER + CONV_DIM + DT_PAD
VMEM_LIMIT = 52 * 1024 * 1024


def _dot(a, b):
    return jnp.dot(a, b, preferred_element_type=F32)


def _dot_nt(a, b):
    return lax.dot_general(a, b, (((1,), (1,)), ((), ())), preferred_element_type=F32)


def _split3(v):
    hi = v.astype(BF16)
    r1 = v - hi.astype(F32)
    mid = r1.astype(BF16)
    lo = (r1 - mid.astype(F32)).astype(BF16)
    return hi, mid, lo


def _dot_f32_rhs(a_bf16_exact, v):
    hi, mid, lo = _split3(v)
    return _dot(a_bf16_exact, hi) + _dot(a_bf16_exact, mid) + _dot(a_bf16_exact, lo)


def _dot_f32_lhs(v, b_bf16_exact):
    hi, mid, lo = _split3(v)
    return _dot(hi, b_bf16_exact) + _dot(mid, b_bf16_exact) + _dot(lo, b_bf16_exact)


def _rmsnorm_rows(xf, g_row):
    ms = jnp.mean(xf * xf, axis=-1, keepdims=True)
    return xf * lax.rsqrt(ms + RMS_EPS) * g_row


def _silu(v):
    return v * (1.0 / (1.0 + jnp.exp(-v)))


def _softplus(v):
    return jnp.maximum(v, 0.0) + jnp.log1p(jnp.exp(-jnp.abs(v)))


def _ssd_chunk(r0, xc_ref, dt_ref, z_ref, st_ref, y_scr, y_ref, e_ref,
               dtb_row, a_row, dskip_row, ng_row):
    rows = pl.ds(r0, CHUNK)
    tri_r = lax.broadcasted_iota(jnp.int32, (CHUNK, CHUNK), 0)
    tri_c = lax.broadcasted_iota(jnp.int32, (CHUNK, CHUNK), 1)
    causal = tri_r >= tri_c
    ltri = causal.astype(BF16)
    lane_lo = tri_c < HEAD_DIM

    dt = _softplus(dt_ref[rows, :] + dtb_row)
    adt = dt * a_row
    acum = _dot_f32_rhs(ltri, adt)
    acum_t = acum.T
    dt_t = dt.T
    a_last = acum[CHUNK - 1:CHUNK, :]
    w_t = (dt * jnp.exp(a_last - acum)).T
    cdec = jnp.exp(a_last)
    cdec_e = _dot_f32_lhs(jnp.broadcast_to(cdec, (8, DT_PAD)), e_ref[...])[0:1, :]

    for g in range(N_GROUPS):
        b_g = xc_ref[rows, D_INNER + g * D_STATE:D_INNER + (g + 1) * D_STATE]
        c_g = xc_ref[rows, D_INNER + N_GROUPS * D_STATE + g * D_STATE:
                     D_INNER + N_GROUPS * D_STATE + (g + 1) * D_STATE]
        b_bf = b_g.astype(BF16)
        c_bf = c_g.astype(BF16)
        cb = _dot_nt(c_bf, b_bf)
        b_t = b_g.T
        gcols = slice(g * HEADS_PER_SSM_GROUP * HEAD_DIM, (g + 1) * HEADS_PER_SSM_GROUP * HEAD_DIM)
        y_off = _dot(c_bf, st_ref[:, gcols].astype(BF16))
        for pr in range(HEADS_PER_SSM_GROUP // 2):
            h0 = g * HEADS_PER_SSM_GROUP + 2 * pr
            pcols = slice(h0 * HEAD_DIM, (h0 + 2) * HEAD_DIM)
            xs_pair = xc_ref[rows, pcols].astype(BF16)
            xs_halves = (jnp.where(lane_lo, xs_pair, jnp.zeros_like(xs_pair)),
                         jnp.where(lane_lo, jnp.zeros_like(xs_pair), xs_pair))
            y_pair = None
            s_pair = None
            a_b = []
            for k in range(2):
                h = h0 + k
                ab = jnp.broadcast_to(acum[:, h:h + 1], (CHUNK, CHUNK))
                a_b.append(ab)
                dec = jnp.exp(jnp.where(causal, ab - acum_t[h:h + 1, :], NEG_BIG))
                wmat = (cb * dec * dt_t[h:h + 1, :]).astype(BF16)
                yd = _dot(wmat, xs_halves[k])
                y_pair = yd if y_pair is None else y_pair + yd
                su = _dot((b_t * w_t[h:h + 1, :]).astype(BF16), xs_halves[k])
                s_pair = su if s_pair is None else s_pair + su
            ea_pair = jnp.exp(jnp.where(lane_lo, a_b[0], a_b[1]))
            lo = (2 * pr) * HEAD_DIM
            y_scr[:, pcols] = y_pair + y_off[:, lo:lo + 2 * HEAD_DIM] * ea_pair
            st_ref[:, pcols] = st_ref[:, pcols] * cdec_e[:, pcols] + s_pair

    gw = D_INNER // N_GROUPS
    for g in range(N_GROUPS):
        cols = slice(g * gw, (g + 1) * gw)
        y = y_scr[:, cols] + dskip_row[:, cols] * xc_ref[rows, cols]
        y = y * _silu(z_ref[rows, cols])
        y = _rmsnorm_rows(y, ng_row[:, cols])
        y_ref[0, rows, cols] = y.astype(y_ref.dtype)


def _ssd_prompt_kernel(x_ref, g1_ref, w_ref, cw_ref, cb_ref, dtb_ref, alog_ref, dskip_ref, ng_ref,
                       e_ref, y_ref, st_out_ref, conv_out_ref,
                       ext_ref, xc_ref, z_ref, dt_ref, st_ref, y_scr, *, tl):
    j = pl.program_id(1)
    nj = pl.num_programs(1)

    @pl.when(j == 0)
    def _():
        st_ref[...] = jnp.zeros_like(st_ref)
        ext_ref[0:8, :] = jnp.zeros((8, CONV_DIM), F32)

    @pl.when(j > 0)
    def _():
        ext_ref[0:8, :] = ext_ref[tl:tl + 8, :]

    h = _rmsnorm_rows(x_ref[0], g1_ref[...]).astype(BF16)
    z_ref[...] = _dot(h, w_ref[:, 0:D_INNER])
    ext_ref[8:8 + tl, :] = _dot(h, w_ref[:, D_INNER:D_INNER + CONV_DIM])
    dt_ref[...] = _dot(h, w_ref[:, D_INNER + CONV_DIM:W_SSD_COLS])

    @pl.when(j == nj - 1)
    def _():
        conv_out_ref[0] = ext_ref[8 + tl - (CONV_W - 1):8 + tl, :]

    cblk = 512
    for rb in range(tl // CHUNK):
        for cbk in range(CONV_DIM // cblk):
            cols = slice(cbk * cblk, (cbk + 1) * cblk)
            acc = cb_ref[:, cols] + cw_ref[0:1, cols] * ext_ref[5 + rb * CHUNK:5 + (rb + 1) * CHUNK, cols]
            for k in range(1, CONV_W):
                acc = acc + cw_ref[k:k + 1, cols] * ext_ref[5 + k + rb * CHUNK:5 + k + (rb + 1) * CHUNK, cols]
            xc_ref[rb * CHUNK:(rb + 1) * CHUNK, cols] = _silu(acc)

    dtb_row = dtb_ref[...]
    a_row = -jnp.exp(alog_ref[...])
    dskip_row = dskip_ref[...]
    ng_row = ng_ref[...]

    def body(c, carry):
        r0 = pl.multiple_of(c * CHUNK, CHUNK)
        _ssd_chunk(r0, xc_ref, dt_ref, z_ref, st_ref, y_scr, y_ref, e_ref,
                   dtb_row, a_row, dskip_row, ng_row)
        return carry

    lax.fori_loop(0, tl // CHUNK, body, 0)

    @pl.when(j == nj - 1)
    def _():
        for k in range(D_INNER // D_STATE):
            st_out_ref[0, k * D_STATE:(k + 1) * D_STATE, :] = st_ref[:, k * D_STATE:(k + 1) * D_STATE].T


def _ssd_prompt(x, g1, w_ssd, conv_w, conv_b, dtb, alog, dskip_e, ng, expand, *, tl=256):
    n, s, _ = x.shape
    kern = functools.partial(_ssd_prompt_kernel, tl=tl)
    const = lambda b, j: (0, 0)
    return pl.pallas_call(
        kern,
        grid=(n, s // tl),
        in_specs=[
            pl.BlockSpec((1, tl, D_MODEL), lambda b, j: (b, j, 0)),
            pl.BlockSpec((1, D_MODEL), const),
            pl.BlockSpec((D_MODEL, W_SSD_COLS), const),
            pl.BlockSpec((CONV_W, CONV_DIM), const),
            pl.BlockSpec((1, CONV_DIM), const),
            pl.BlockSpec((1, DT_PAD), const),
            pl.BlockSpec((1, DT_PAD), const),
            pl.BlockSpec((1, D_INNER), const),
            pl.BlockSpec((1, D_INNER), const),
            pl.BlockSpec((DT_PAD, D_INNER), const),
        ],
        out_specs=[
            pl.BlockSpec((1, tl, D_INNER), lambda b, j: (b, j, 0)),
            pl.BlockSpec((1, D_INNER, D_STATE), lambda b, j: (b, 0, 0)),
            pl.BlockSpec((1, CONV_W - 1, CONV_DIM), lambda b, j: (b, 0, 0)),
        ],
        out_shape=[
            jax.ShapeDtypeStruct((n, s, D_INNER), BF16),
            jax.ShapeDtypeStruct((n, D_INNER, D_STATE), F32),
            jax.ShapeDtypeStruct((n, CONV_W - 1, CONV_DIM), F32),
        ],
        scratch_shapes=[
            pltpu.VMEM((tl + 8, CONV_DIM), F32),
            pltpu.VMEM((tl, CONV_DIM), F32),
            pltpu.VMEM((tl, D_INNER), F32),
            pltpu.VMEM((tl, DT_PAD), F32),
            pltpu.VMEM((D_STATE, D_INNER), F32),
            pltpu.VMEM((CHUNK, D_INNER), F32),
        ],
        compiler_params=pltpu.CompilerParams(
            dimension_semantics=("parallel", "arbitrary"), vmem_limit_bytes=VMEM_LIMIT),
        name="ssd_prompt",
    )(x, g1, w_ssd, conv_w, conv_b, dtb, alog, dskip_e, ng, expand)


def _prep_ssd_params(norm1_g, w_in, conv_w, conv_b, dt_bias, a_log, d_skip, ssm_norm_g):
    w_ssd = jnp.concatenate(
        [w_in[:, 0:OFF_DT], w_in[:, OFF_DT:OFF_Q], jnp.zeros((D_MODEL, DT_PAD - N_HEADS), w_in.dtype)],
        axis=1).astype(BF16)
    pad = lambda v: jnp.pad(v, (0, DT_PAD - N_HEADS)).reshape(1, DT_PAD)
    expand = (jnp.arange(DT_PAD)[:, None] == (jnp.arange(D_INNER) // HEAD_DIM)[None, :]).astype(BF16)
    return dict(
        g1=norm1_g.reshape(1, D_MODEL), w_ssd=w_ssd, conv_w=conv_w, conv_b=conv_b.reshape(1, CONV_DIM),
        dtb=pad(dt_bias), alog=pad(a_log), dskip_e=jnp.repeat(d_skip, HEAD_DIM).reshape(1, D_INNER),
        ng=ssm_norm_g.reshape(1, D_INNER), expand=expand)


QKV_COLS = 9 * ATTN_WIDTH
KV_WIDTH = 2 * ATTN_WIDTH


def _qkv_prompt_kernel(x_ref, g1_ref, w_ref, q0_ref, q1_ref, q2_ref, kv0_ref, kv1_ref, kv2_ref,
                       t0_ref, t1_ref, t2_ref, *, tm):
    j = pl.program_id(1)
    nj = pl.num_programs(1)
    h = _rmsnorm_rows(x_ref[0], g1_ref[...]).astype(BF16)
    q = _dot(h, w_ref[:, 0:3 * ATTN_WIDTH]) * (ATTN_DIM ** -0.5)
    for g, q_ref in enumerate((q0_ref, q1_ref, q2_ref)):
        q_ref[0] = q[:, g * ATTN_WIDTH:(g + 1) * ATTN_WIDTH].astype(BF16)
    for g, (kv_ref, t_ref) in enumerate(((kv0_ref, t0_ref), (kv1_ref, t1_ref), (kv2_ref, t2_ref))):
        lo = 3 * ATTN_WIDTH + g * KV_WIDTH
        kv = _dot(h, w_ref[:, lo:lo + KV_WIDTH])
        kv_ref[0] = kv.astype(BF16)
        keep = WINDOWS[g]
        if keep >= tm:
            @pl.when(j >= nj - keep // tm)
            def _():
                t_ref[0] = kv
        else:
            @pl.when(j == nj - 1)
            def _():
                t_ref[0] = kv[tm - keep:, :]


def _qkv_prompt(x, g1, w_qkv, *, tm=512):
    n, s, _ = x.shape
    nj = s // tm
    const = lambda b, j: (0, 0)
    row = lambda b, j: (b, j, 0)

    def tail_spec(keep):
        if keep >= tm:
            first = nj - keep // tm
            return pl.BlockSpec((1, tm, KV_WIDTH), lambda b, j: (b, jnp.maximum(j - first, 0), 0))
        return pl.BlockSpec((1, keep, KV_WIDTH), lambda b, j: (b, 0, 0))

    return pl.pallas_call(
        functools.partial(_qkv_prompt_kernel, tm=tm),
        grid=(n, nj),
        in_specs=[pl.BlockSpec((1, tm, D_MODEL), row), pl.BlockSpec((1, D_MODEL), const),
                  pl.BlockSpec((D_MODEL, QKV_COLS), const)],
        out_specs=[pl.BlockSpec((1, tm, ATTN_WIDTH), row)] * 3 + [pl.BlockSpec((1, tm, KV_WIDTH), row)] * 3
        + [tail_spec(w) for w in WINDOWS],
        out_shape=[jax.ShapeDtypeStruct((n, s, ATTN_WIDTH), BF16)] * 3
        + [jax.ShapeDtypeStruct((n, s, KV_WIDTH), BF16)] * 3
        + [jax.ShapeDtypeStruct((n, w, KV_WIDTH), F32) for w in WINDOWS],
        compiler_params=pltpu.CompilerParams(
            dimension_semantics=("parallel", "arbitrary"), vmem_limit_bytes=VMEM_LIMIT),
        name="qkv_prompt",
    )(x, g1, w_qkv)


def _alibi_slope(group, head):
    return 2.0 ** (-8.0 * (group * ATTN_HEADS + head + 1) / (3 * ATTN_HEADS))


def _attn_prompt_kernel(q_ref, kv_ref, kvp_ref, o_ref, l_ref, kv_scr, *, group, qb):
    dil = DILATIONS[group]
    band = WINDOWS[group] // dil
    blk = ATTN_BLOCK
    jb = pl.program_id(2)
    kv_scr[0:blk, :] = kvp_ref[0]
    kv_scr[blk:, :] = kv_ref[0]
    qi = lax.broadcasted_iota(jnp.int32, (blk, 2 * blk), 0) + blk
    kj = lax.broadcasted_iota(jnp.int32, (blk, 2 * blk), 1)
    delta = qi - kj
    valid = (delta >= 0) & (delta <= band)
    valid_first = valid & ((jb > 0) | (kj >= blk))
    dist = (dil * delta).astype(F32)
    lane_lo_q = lax.broadcasted_iota(jnp.int32, (blk, 2 * ATTN_DIM), 1) < ATTN_DIM
    lane_lo_v = lax.broadcasted_iota(jnp.int32, (2 * blk, 2 * ATTN_DIM), 1) < ATTN_DIM
    for i in range(qb):
        rows = slice(i * blk, (i + 1) * blk)
        krows = slice(i * blk, (i + 2) * blk)
        ok = valid_first if i == 0 else valid
        for pr in range(ATTN_HEADS // 2):
            cols = slice(pr * 2 * ATTN_DIM, (pr + 1) * 2 * ATTN_DIM)
            vcols = slice(ATTN_WIDTH + pr * 2 * ATTN_DIM, ATTN_WIDTH + (pr + 1) * 2 * ATTN_DIM)
            q_pair = q_ref[0, rows, cols]
            k_pair = kv_scr[krows, cols]
            v_pair = kv_scr[krows, vcols]
            o_pair = None
            lse = []
            for k in range(2):
                slope = _alibi_slope(group, 2 * pr + k)
                sel_q = lane_lo_q if k == 0 else ~lane_lo_q
                sel_v = lane_lo_v if k == 0 else ~lane_lo_v
                q_h = jnp.where(sel_q, q_pair, jnp.zeros_like(q_pair))
                v_h = jnp.where(sel_v, v_pair, jnp.zeros_like(v_pair))
                s = _dot_nt(q_h, k_pair)
                s = jnp.where(ok, s - slope * dist, NEG_BIG)
                m = jnp.max(s, axis=-1, keepdims=True)
                p = jnp.exp(s - m)
                den = jnp.sum(p, axis=-1, keepdims=True)
                o_h = _dot(p.astype(BF16), v_h) / den
                o_pair = o_h if o_pair is None else o_pair + o_h
                lse.append(m + jnp.log(den))
            o_ref[0, rows, cols] = o_pair.astype(o_ref.dtype)
            l_ref[0, rows, cols] = jnp.where(lane_lo_q, lse[0], lse[1])


def _attn_prompt(q, kv, *, group, qb=4):
    n, s, _ = q.shape
    dil = DILATIONS[group]
    ls = s // dil
    qb = min(qb, ls // ATTN_BLOCK)
    rows = qb * ATTN_BLOCK
    qv = q.reshape(n, ls, dil * ATTN_WIDTH)
    kvv = kv.reshape(n, ls, dil * KV_WIDTH)
    own = lambda b, r, jb: (b, jb, r)
    prev = lambda b, r, jb: (b, jnp.maximum(jb * qb - 1, 0), r)
    o, lse = pl.pallas_call(
        functools.partial(_attn_prompt_kernel, group=group, qb=qb),
        grid=(n, dil, ls // rows),
        in_specs=[pl.BlockSpec((1, rows, ATTN_WIDTH), own), pl.BlockSpec((1, rows, KV_WIDTH), own),
                  pl.BlockSpec((1, ATTN_BLOCK, KV_WIDTH), prev)],
        out_specs=[pl.BlockSpec((1, rows, ATTN_WIDTH), own)] * 2,
        out_shape=[jax.ShapeDtypeStruct((n, ls, dil * ATTN_WIDTH), BF16),
                   jax.ShapeDtypeStruct((n, ls, dil * ATTN_WIDTH), F32)],
        scratch_shapes=[pltpu.VMEM((rows + ATTN_BLOCK, KV_WIDTH), BF16)],
        compiler_params=pltpu.CompilerParams(
            dimension_semantics=("parallel", "parallel", "arbitrary"), vmem_limit_bytes=VMEM_LIMIT),
        name=f"attn_prompt_g{group}",
    )(qv, kvv, kvv)
    return o.reshape(n, s, ATTN_WIDTH), lse.reshape(n, s, ATTN_WIDTH)


def _merge_kernel(*refs, combine):
    if combine:
        (x_ref, ys_ref, o0_ref, o1_ref, o2_ref, l0_ref, l1_ref, l2_ref,
         g1_ref, wg_ref, ws_ref, wa_ref, wo_ref, out_ref) = refs
        l0, l1, l2 = l0_ref[...], l1_ref[...], l2_ref[...]
        m = jnp.maximum(jnp.maximum(l0, l1), l2)
        e0, e1, e2 = jnp.exp(l0 - m), jnp.exp(l1 - m), jnp.exp(l2 - m)
        o = (e0 * o0_ref[...].astype(F32) + e1 * o1_ref[...].astype(F32)
             + e2 * o2_ref[...].astype(F32)) / (e0 + e1 + e2)
    else:
        x_ref, ys_ref, o_ref, g1_ref, wg_ref, ws_ref, wa_ref, wo_ref, out_ref = refs
        o = o_ref[...]
    xf = x_ref[...]
    h = _rmsnorm_rows(xf, g1_ref[...]).astype(BF16)
    gates = _dot(h, wg_ref[...])
    gates = 1.0 / (1.0 + jnp.exp(-gates))
    merged = (gates[:, 0:D_MODEL] * _dot(ys_ref[...], ws_ref[...])
              + gates[:, D_MODEL:2 * D_MODEL] * _dot(o.astype(BF16), wa_ref[...]))
    out_ref[...] = xf + _dot(merged.astype(BF16), wo_ref[...])


def _merge(x2d, y_ssm, attn, g1, w_gates, w_ssm_out, w_attn_out, w_o, *, tm):
    m = x2d.shape[0]
    combine = len(attn) == 6
    row = lambda i: (i, 0)
    const = lambda i: (0, 0)
    act_specs = [pl.BlockSpec((tm, D_MODEL), row), pl.BlockSpec((tm, D_INNER), row)]
    act_specs += [pl.BlockSpec((tm, ATTN_WIDTH), row)] * len(attn)
    w_specs = [pl.BlockSpec((1, D_MODEL), const), pl.BlockSpec((D_MODEL, 2 * D_MODEL), const),
               pl.BlockSpec((D_INNER, D_MODEL), const), pl.BlockSpec((ATTN_WIDTH, D_MODEL), const),
               pl.BlockSpec((D_MODEL, D_MODEL), const)]
    return pl.pallas_call(
        functools.partial(_merge_kernel, combine=combine),
        grid=(m // tm,),
        in_specs=act_specs + w_specs,
        out_specs=pl.BlockSpec((tm, D_MODEL), row),
        out_shape=jax.ShapeDtypeStruct((m, D_MODEL), F32),
        compiler_params=pltpu.CompilerParams(
            dimension_semantics=("parallel",), vmem_limit_bytes=VMEM_LIMIT),
        name="merge",
    )(x2d, y_ssm, *attn, g1, w_gates, w_ssm_out, w_attn_out, w_o)


def _ffn_kernel(x_ref, g2_ref, gf_ref, wg_ref, wu_ref, wd_ref, out_ref, h_scr, acc_scr):
    f = pl.program_id(1)

    @pl.when(f == 0)
    def _():
        h_scr[...] = _rmsnorm_rows(x_ref[...], g2_ref[...]).astype(BF16)
        acc_scr[...] = jnp.zeros_like(acc_scr)

    h = h_scr[...]
    gate = _dot(h, wg_ref[...])
    act = gate * (1.0 / (1.0 + jnp.exp(-gate))) * _dot(h, wu_ref[...])
    acc_scr[...] += _dot(act.astype(BF16), wd_ref[...])

    @pl.when(f == pl.num_programs(1) - 1)
    def _():
        out_ref[...] = _rmsnorm_rows(x_ref[...] + acc_scr[...], gf_ref[...])


def _ffn(x2d, g2, gf, w_gate, w_up, w_down, *, tm, tf=256):
    m = x2d.shape[0]
    row = lambda i, f: (i, 0)
    const = lambda i, f: (0, 0)
    return pl.pallas_call(
        _ffn_kernel,
        grid=(m // tm, D_FF // tf),
        in_specs=[pl.BlockSpec((tm, D_MODEL), row), pl.BlockSpec((1, D_MODEL), const),
                  pl.BlockSpec((1, D_MODEL), const),
                  pl.BlockSpec((D_MODEL, tf), lambda i, f: (0, f)),
                  pl.BlockSpec((D_MODEL, tf), lambda i, f: (0, f)),
                  pl.BlockSpec((tf, D_MODEL), lambda i, f: (f, 0))],
        out_specs=pl.BlockSpec((tm, D_MODEL), row),
        out_shape=jax.ShapeDtypeStruct((m, D_MODEL), F32),
        scratch_shapes=[pltpu.VMEM((tm, D_MODEL), BF16), pltpu.VMEM((tm, D_MODEL), F32)],
        compiler_params=pltpu.CompilerParams(
            dimension_semantics=("parallel", "arbitrary"), vmem_limit_bytes=VMEM_LIMIT),
        name="ffn",
    )(x2d, g2, gf, w_gate, w_up, w_down)


SAMPLE_BLOCK = 8


def _ssd_sample_pre_kernel(x_ref, g1_ref, w_ref, cs_ref, cw_ref, cb_ref, dtb_ref, alog_ref, e_ref,
                           z_ref, xs_ref, b_ref, c_ref, xdt_t_ref, da_t_ref, conv_out_ref):
    h = _rmsnorm_rows(x_ref[...], g1_ref[...]).astype(BF16)
    z_ref[...] = _dot(h, w_ref[:, 0:D_INNER])
    xbc = _dot(h, w_ref[:, D_INNER:D_INNER + CONV_DIM])
    dt_raw = _dot(h, w_ref[:, D_INNER + CONV_DIM:W_SSD_COLS])
    acc = cb_ref[...] + cw_ref[CONV_W - 1:CONV_W, :] * xbc
    for k in range(CONV_W - 1):
        hist = cs_ref[:, k * CONV_DIM:(k + 1) * CONV_DIM]
        acc = acc + cw_ref[k:k + 1, :] * hist
        if k > 0:
            conv_out_ref[:, (k - 1) * CONV_DIM:k * CONV_DIM] = hist
    conv_out_ref[:, (CONV_W - 2) * CONV_DIM:(CONV_W - 1) * CONV_DIM] = xbc
    xc = _silu(acc)
    xs = xc[:, 0:D_INNER]
    xs_ref[...] = xs
    b_ref[...] = xc[:, D_INNER:D_INNER + N_GROUPS * D_STATE]
    c_ref[...] = xc[:, D_INNER + N_GROUPS * D_STATE:CONV_DIM]
    dt = _softplus(dt_raw + dtb_ref[...])
    da = jnp.exp(dt * (-jnp.exp(alog_ref[...])))
    xdt = xs * _dot_f32_lhs(dt, e_ref[...])
    da_e = _dot_f32_lhs(da, e_ref[...])
    nb = x_ref.shape[0]
    for k in range(D_INNER // nb):
        xdt_t_ref[k * nb:(k + 1) * nb, :] = xdt[:, k * nb:(k + 1) * nb].T
        da_t_ref[k * nb:(k + 1) * nb, :] = da_e[:, k * nb:(k + 1) * nb].T


def _ssd_sample_pre(x2d, conv_state2d, p):
    nb = x2d.shape[0]
    full = lambda shape: pl.BlockSpec(shape, lambda i: (0,) * len(shape))
    return pl.pallas_call(
        _ssd_sample_pre_kernel,
        grid=(1,),
        in_specs=[full((nb, D_MODEL)), full((1, D_MODEL)), full((D_MODEL, W_SSD_COLS)),
                  full((nb, (CONV_W - 1) * CONV_DIM)), full((CONV_W, CONV_DIM)), full((1, CONV_DIM)),
                  full((1, DT_PAD)), full((1, DT_PAD)), full((DT_PAD, D_INNER))],
        out_specs=[full((nb, D_INNER)), full((nb, D_INNER)), full((nb, N_GROUPS * D_STATE)),
                   full((nb, N_GROUPS * D_STATE)), full((D_INNER, nb)), full((D_INNER, nb)),
                   full((nb, (CONV_W - 1) * CONV_DIM))],
        out_shape=[jax.ShapeDtypeStruct((nb, D_INNER), F32), jax.ShapeDtypeStruct((nb, D_INNER), F32),
                   jax.ShapeDtypeStruct((nb, N_GROUPS * D_STATE), F32),
                   jax.ShapeDtypeStruct((nb, N_GROUPS * D_STATE), F32),
                   jax.ShapeDtypeStruct((D_INNER, nb), F32), jax.ShapeDtypeStruct((D_INNER, nb), F32),
                   jax.ShapeDtypeStruct((nb, (CONV_W - 1) * CONV_DIM), F32)],
        compiler_params=pltpu.CompilerParams(vmem_limit_bytes=VMEM_LIMIT),
        name="ssd_sample_pre",
    )(x2d, p["g1"], p["w_ssd"], conv_state2d, p["conv_w"], p["conv_b"], p["dtb"], p["alog"], p["expand"])


def _ssd_sample_state_kernel(st_ref, xdt_ref, da_ref, b_ref, c_ref, xs_ref, z_ref, dskip_ref, ng_ref,
                             st_out_ref, y_ref, y_scr):
    grows = D_INNER // N_GROUPS
    for i in range(SAMPLE_BLOCK):
        for g in range(N_GROUPS):
            rows = slice(g * grows, (g + 1) * grows)
            s_new = (st_ref[i, rows, :] * da_ref[0, rows, i:i + 1]
                     + xdt_ref[0, rows, i:i + 1] * b_ref[i, g:g + 1, :])
            st_out_ref[i, rows, :] = s_new
            y_t = _dot_nt(c_ref[i].astype(BF16), s_new.astype(BF16))
            y_scr[i:i + 1, rows] = y_t[g:g + 1, :]
    for g in range(N_GROUPS):
        cols = slice(g * grows, (g + 1) * grows)
        y = y_scr[:, cols] + dskip_ref[:, cols] * xs_ref[:, cols]
        y = y * _silu(z_ref[:, cols])
        y_ref[:, cols] = _rmsnorm_rows(y, ng_ref[:, cols]).astype(y_ref.dtype)


def _ssd_sample_state(state, xdt3, da3, b3, c3, xs, z, p):
    nb = state.shape[0]
    bb = SAMPLE_BLOCK
    const = lambda i: (0, 0)
    return pl.pallas_call(
        _ssd_sample_state_kernel,
        grid=(nb // bb,),
        in_specs=[pl.BlockSpec((bb, D_INNER, D_STATE), lambda i: (i, 0, 0)),
                  pl.BlockSpec((1, D_INNER, bb), lambda i: (i, 0, 0)),
                  pl.BlockSpec((1, D_INNER, bb), lambda i: (i, 0, 0)),
                  pl.BlockSpec((bb, N_GROUPS, D_STATE), lambda i: (i, 0, 0)),
                  pl.BlockSpec((bb, N_GROUPS, D_STATE), lambda i: (i, 0, 0)),
                  pl.BlockSpec((bb, D_INNER), lambda i: (i, 0)),
                  pl.BlockSpec((bb, D_INNER), lambda i: (i, 0)),
                  pl.BlockSpec((1, D_INNER), const), pl.BlockSpec((1, D_INNER), const)],
        out_specs=[pl.BlockSpec((bb, D_INNER, D_STATE), lambda i: (i, 0, 0)),
                   pl.BlockSpec((bb, D_INNER), lambda i: (i, 0))],
        out_shape=[jax.ShapeDtypeStruct((nb, D_INNER, D_STATE), F32),
                   jax.ShapeDtypeStruct((nb, D_INNER), BF16)],
        scratch_shapes=[pltpu.VMEM((bb, D_INNER), F32)],
        compiler_params=pltpu.CompilerParams(
            dimension_semantics=("parallel",), vmem_limit_bytes=VMEM_LIMIT),
        name="ssd_sample_state",
    )(state, xdt3, da3, b3, c3, xs, z, p["dskip_e"], p["ng"])


def _proj_rows_kernel(x_ref, g1_ref, w_ref, out_ref):
    h = _rmsnorm_rows(x_ref[...], g1_ref[...]).astype(BF16)
    out_ref[...] = _dot(h, w_ref[...])


def _proj_rows(x2d, g1, w):
    nb, cols = x2d.shape[0], w.shape[1]
    full = lambda shape: pl.BlockSpec(shape, lambda i: (0,) * len(shape))
    return pl.pallas_call(
        _proj_rows_kernel,
        grid=(1,),
        in_specs=[full((nb, D_MODEL)), full((1, D_MODEL)), full((D_MODEL, cols))],
        out_specs=full((nb, cols)),
        out_shape=jax.ShapeDtypeStruct((nb, cols), F32),
        compiler_params=pltpu.CompilerParams(vmem_limit_bytes=VMEM_LIMIT),
        name="proj_rows",
    )(x2d, g1, w)


def _attn_sample_kernel(q_ref, kvn_ref, c0_ref, c1_ref, c2_ref, o_ref):
    nkeys = ATTN_BLOCK
    head_of_lane = lax.broadcasted_iota(jnp.int32, (8, ATTN_WIDTH), 1) // ATTN_DIM
    row_id = lax.broadcasted_iota(jnp.int32, (8, ATTN_WIDTH), 0)
    head_mask = head_of_lane == row_id
    key_row = lax.broadcasted_iota(jnp.int32, (8, nkeys), 1)
    row_h = lax.broadcasted_iota(jnp.int32, (8, nkeys), 0)
    for i in range(SAMPLE_BLOCK):
        s_cache, s_new, vals = [], [], []
        for g, c_ref in enumerate((c0_ref, c1_ref, c2_ref)):
            dil = DILATIONS[g]
            q_g = q_ref[i:i + 1, g * ATTN_WIDTH:(g + 1) * ATTN_WIDTH] * (ATTN_DIM ** -0.5)
            q_bd = jnp.where(head_mask, jnp.broadcast_to(q_g, (8, ATTN_WIDTH)), 0.0).astype(BF16)
            k_new = kvn_ref[i:i + 1, g * KV_WIDTH:g * KV_WIDTH + ATTN_WIDTH].astype(BF16)
            v_new = kvn_ref[i:i + 1, g * KV_WIDTH + ATTN_WIDTH:(g + 1) * KV_WIDTH].astype(BF16)
            k_c = c_ref[i, :, 0:ATTN_WIDTH].astype(BF16)
            v_c = c_ref[i, :, ATTN_WIDTH:KV_WIDTH].astype(BF16)
            slope = jnp.zeros((8, nkeys), F32)
            for hh in range(ATTN_HEADS):
                slope = jnp.where(row_h == hh, _alibi_slope(g, hh), slope)
            dist = (dil * (nkeys - key_row)).astype(F32)
            s_cache.append(_dot_nt(q_bd, k_c) - slope * dist)
            s_new.append(jnp.sum(q_bd.astype(F32) * k_new.astype(F32), axis=-1, keepdims=True))
            vals.append((v_c, v_new))
        m = s_new[0]
        for g in range(3):
            m = jnp.maximum(m, jnp.maximum(s_new[g], jnp.max(s_cache[g], axis=-1, keepdims=True)))
        den = jnp.zeros((8, 1), F32)
        acc = jnp.zeros((8, ATTN_WIDTH), F32)
        for g in range(3):
            p_c = jnp.exp(s_cache[g] - m)
            p_n = jnp.exp(s_new[g] - m)
            den = den + jnp.sum(p_c, axis=-1, keepdims=True) + p_n
            v_c, v_new = vals[g]
            acc = acc + _dot(p_c.astype(BF16), v_c) + p_n.astype(BF16).astype(F32) * v_new.astype(F32)
        o_full = jnp.where(head_mask, acc / den, 0.0)
        o_ref[i:i + 1, :] = jnp.sum(o_full, axis=0, keepdims=True)


def _attn_sample(q, kv_new, caches):
    nb = q.shape[0]
    bb = SAMPLE_BLOCK
    views = []
    for g, c in enumerate(caches):
        dil = DILATIONS[g]
        lc = c.shape[1]
        assert lc == WINDOWS[g]
        views.append(c.reshape(nb, lc // dil, dil * KV_WIDTH))
    cache_spec = pl.BlockSpec((bb, ATTN_BLOCK, KV_WIDTH), lambda i: (i, 0, 0))
    return pl.pallas_call(
        _attn_sample_kernel,
        grid=(nb // bb,),
        in_specs=[pl.BlockSpec((bb, 3 * ATTN_WIDTH), lambda i: (i, 0)),
                  pl.BlockSpec((bb, 3 * KV_WIDTH), lambda i: (i, 0))] + [cache_spec] * 3,
        out_specs=pl.BlockSpec((bb, ATTN_WIDTH), lambda i: (i, 0)),
        out_shape=jax.ShapeDtypeStruct((nb, ATTN_WIDTH), F32),
        compiler_params=pltpu.CompilerParams(
            dimension_semantics=("parallel",), vmem_limit_bytes=VMEM_LIMIT),
        name="attn_sample",
    )(q, kv_new, *views)


def _layer_sample(x2d, state_ssm, state_conv, caches, wp):
    nb = x2d.shape[0]
    p = wp["ssd"]
    bb = SAMPLE_BLOCK
    z, xs, bm, cm, xdt_t, da_t, conv_new = _ssd_sample_pre(
        x2d, state_conv.reshape(nb, (CONV_W - 1) * CONV_DIM), p)
    to_blocks = lambda t: t.reshape(D_INNER, nb // bb, bb).transpose(1, 0, 2)
    st_new, y_ssm = _ssd_sample_state(
        state_ssm.reshape(nb, D_INNER, D_STATE), to_blocks(xdt_t), to_blocks(da_t),
        bm.reshape(nb, N_GROUPS, D_STATE), cm.reshape(nb, N_GROUPS, D_STATE), xs, z, p)
    qkv = _proj_rows(x2d, p["g1"], wp["w_qkv"])
    q, kv_new = qkv[:, 0:3 * ATTN_WIDTH], qkv[:, 3 * ATTN_WIDTH:]
    o = _attn_sample(q, kv_new, [c.reshape(nb, c.shape[1], KV_WIDTH) for c in caches])
    x1 = _merge(x2d, y_ssm, [o], p["g1"], wp["w_gates"], wp["w_ssm_out"], wp["w_attn_out"], wp["w_o"], tm=nb)
    y = _ffn(x1, wp["g2"], wp["gf"], wp["w_ffn_gate"], wp["w_ffn_up"], wp["w_ffn_down"], tm=nb)
    return y, st_new, conv_new, kv_new


def _prep_attn_weight(w_in):
    cols = [w_in[:, OFF_Q:OFF_Q + 3 * ATTN_WIDTH]]
    for g in range(3):
        cols.append(w_in[:, OFF_K + g * ATTN_WIDTH:OFF_K + (g + 1) * ATTN_WIDTH])
        cols.append(w_in[:, OFF_V + g * ATTN_WIDTH:OFF_V + (g + 1) * ATTN_WIDTH])
    return jnp.concatenate(cols, axis=1).astype(BF16)


def _layer_prompt(x, wp):
    n, s, _ = x.shape
    y_ssm, st, conv = _ssd_prompt(x, **wp["ssd"])
    q0, q1, q2, kv0, kv1, kv2, t0, t1, t2 = _qkv_prompt(x, wp["ssd"]["g1"], wp["w_qkv"])
    outs, lses = [], []
    for g, (q, kv) in enumerate(((q0, kv0), (q1, kv1), (q2, kv2))):
        o, lse = _attn_prompt(q, kv, group=g)
        outs.append(o.reshape(n * s, ATTN_WIDTH))
        lses.append(lse.reshape(n * s, ATTN_WIDTH))
    x1 = _merge(x.reshape(n * s, D_MODEL), y_ssm.reshape(n * s, D_INNER), outs + lses, wp["ssd"]["g1"],
                wp["w_gates"], wp["w_ssm_out"], wp["w_attn_out"], wp["w_o"], tm=256)
    y = _ffn(x1, wp["g2"], wp["gf"], wp["w_ffn_gate"], wp["w_ffn_up"], wp["w_ffn_down"], tm=1024)
    return y.reshape(n, s, D_MODEL), st, conv, (t0, t1, t2)


def _prep_weights(norm1_g, w_in, conv_w, conv_b, dt_bias, a_log, d_skip, ssm_norm_g, w_ssm_out, w_attn_out,
                  w_o, norm2_g, w_ffn_gate, w_ffn_up, w_ffn_down, norm_f_g):
    return dict(
        ssd=_prep_ssd_params(norm1_g, w_in, conv_w, conv_b, dt_bias, a_log, d_skip, ssm_norm_g),
        w_qkv=_prep_attn_weight(w_in),
        w_gates=w_in[:, OFF_GS:OFF_GS + 2 * D_MODEL].astype(BF16),
        w_ssm_out=w_ssm_out.astype(BF16), w_attn_out=w_attn_out.astype(BF16), w_o=w_o.astype(BF16),
        g2=norm2_g.reshape(1, D_MODEL), gf=norm_f_g.reshape(1, D_MODEL),
        w_ffn_gate=w_ffn_gate.astype(BF16), w_ffn_up=w_ffn_up.astype(BF16),
        w_ffn_down=w_ffn_down.astype(BF16))


def kernel(x_prompt, x_sample, state_ssm, state_conv, cache_kv_w128, cache_kv_w512, cache_kv_w2048, norm1_g, w_in, conv_w, conv_b, dt_bias, a_log, d_skip, ssm_norm_g, w_ssm_out, w_attn_out, w_o, norm2_g, w_ffn_gate, w_ffn_up, w_ffn_down, norm_f_g):
    wp = _prep_weights(norm1_g[0], w_in[0], conv_w[0], conv_b[0], dt_bias[0], a_log[0], d_skip[0],
                       ssm_norm_g[0], w_ssm_out[0], w_attn_out[0], w_o[0], norm2_g[0], w_ffn_gate[0],
                       w_ffn_up[0], w_ffn_down[0], norm_f_g)
    n, s, _ = x_prompt.shape
    nb = x_sample.shape[0]
    y_p, st_p, conv_p, tails = _layer_prompt(x_prompt, wp)
    y_s, st_s, conv_s, kv_s = _layer_sample(
        x_sample.reshape(nb, D_MODEL), state_ssm[0], state_conv[0],
        (cache_kv_w128[0], cache_kv_w512[0], cache_kv_w2048[0]), wp)
    kv_tail_shape = (2, ATTN_HEADS, ATTN_DIM)
    outs = [y_p, y_s.reshape(nb, 1, D_MODEL),
            st_p.reshape(1, n, N_HEADS, HEAD_DIM, D_STATE), st_s.reshape(1, nb, N_HEADS, HEAD_DIM, D_STATE),
            conv_p.reshape(1, n, CONV_W - 1, CONV_DIM), conv_s.reshape(1, nb, CONV_W - 1, CONV_DIM)]
    for g in range(3):
        outs.append(tails[g].reshape((1, n, WINDOWS[g]) + kv_tail_shape))
        outs.append(kv_s[:, g * KV_WIDTH:(g + 1) * KV_WIDTH].reshape((1, nb, 1) + kv_tail_shape))
    return tuple(outs)
```

```python
import functools

import jax
import jax.numpy as jnp
from jax import lax
from jax.experimental import pallas as pl
from jax.experimental.pallas import tpu as pltpu

F32 = jnp.float32
BF16 = jnp.bfloat16

D_MODEL = 1024
D_INNER = 2048
HEAD_DIM = 64
N_HEADS = 32
N_GROUPS = 4
HEADS_PER_SSM_GROUP = N_HEADS // N_GROUPS
D_STATE = 128
CONV_W = 4
CONV_DIM = D_INNER + 2 * N_GROUPS * D_STATE
CHUNK = 128
DT_PAD = 128
ATTN_HEADS = 4
ATTN_DIM = 64
ATTN_WIDTH = ATTN_HEADS * ATTN_DIM
WINDOWS = (128, 512, 2048)
DILATIONS = (1, 4, 16)
ATTN_BLOCK = 128
D_FF = 2816
RMS_EPS = 1e-6
NEG_BIG = -1e30
OFF_XBC = D_INNER
OFF_DT = OFF_XBC + CONV_DIM
OFF_Q = OFF_DT + N_HEADS
OFF_K = OFF_Q + 3 * ATTN_WIDTH
OFF_V = OFF_K + 3 * ATTN_WIDTH
OFF_GS = OFF_V + 3 * ATTN_WIDTH
W_SSD_COLS = D_INNER + CONV_DIM + DT_PAD
VMEM_LIMIT = 52 * 1024 * 1024


def _dot(a, b):
    return jnp.dot(a, b, preferred_element_type=F32)


def _dot_nt(a, b):
    return lax.dot_general(a, b, (((1,), (1,)), ((), ())), preferred_element_type=F32)


def _split3(v):
    hi = v.astype(BF16)
    r1 = v - hi.astype(F32)
    mid = r1.astype(BF16)
    lo = (r1 - mid.astype(F32)).astype(BF16)
    return hi, mid, lo


def _dot_f32_rhs(a_bf16_exact, v):
    hi, mid, lo = _split3(v)
    return _dot(a_bf16_exact, hi) + _dot(a_bf16_exact, mid) + _dot(a_bf16_exact, lo)


def _dot_f32_lhs(v, b_bf16_exact):
    hi, mid, lo = _split3(v)
    return _dot(hi, b_bf16_exact) + _dot(mid, b_bf16_exact) + _dot(lo, b_bf16_exact)


def _rmsnorm_rows(xf, g_row):
    ms = jnp.mean(xf * xf, axis=-1, keepdims=True)
    return xf * lax.rsqrt(ms + RMS_EPS) * g_row


def _silu(v):
    return v * (1.0 / (1.0 + jnp.exp(-v)))


def _softplus(v):
    return jnp.maximum(v, 0.0) + jnp.log1p(jnp.exp(-jnp.abs(v)))


def _ssd_chunk(out_rows, slot, xc_ref, dt_ref, z_ref, st_ref, y_scr, y_ref, e_ref,
               dtb_row, a_row, dskip_row, ng_row):
    rows = slice(None)
    tri_r = lax.broadcasted_iota(jnp.int32, (CHUNK, CHUNK), 0)
    tri_c = lax.broadcasted_iota(jnp.int32, (CHUNK, CHUNK), 1)
    causal = tri_r >= tri_c
    ltri = causal.astype(BF16)
    lane_lo = tri_c < HEAD_DIM

    dt = _softplus(dt_ref[slot] + dtb_row)
    adt = dt * a_row
    acum = _dot_f32_rhs(ltri, adt)
    acum_t = acum.T
    dt_t = dt.T
    a_last = acum[CHUNK - 1:CHUNK, :]
    w_t = (dt * jnp.exp(a_last - acum)).T
    cdec = jnp.exp(a_last)
    cdec_e = _dot_f32_lhs(jnp.broadcast_to(cdec, (8, DT_PAD)), e_ref[...])[0:1, :]

    for g in range(N_GROUPS):
        b_g = xc_ref[rows, D_INNER + g * D_STATE:D_INNER + (g + 1) * D_STATE]
        c_g = xc_ref[rows, D_INNER + N_GROUPS * D_STATE + g * D_STATE:
                     D_INNER + N_GROUPS * D_STATE + (g + 1) * D_STATE]
        b_bf = b_g.astype(BF16)
        c_bf = c_g.astype(BF16)
        cb = _dot_nt(c_bf, b_bf)
        b_t = b_g.T
        gcols = slice(g * HEADS_PER_SSM_GROUP * HEAD_DIM, (g + 1) * HEADS_PER_SSM_GROUP * HEAD_DIM)
        y_off = _dot(c_bf, st_ref[:, gcols].astype(BF16))
        for pr in range(HEADS_PER_SSM_GROUP // 2):
            h0 = g * HEADS_PER_SSM_GROUP + 2 * pr
            pcols = slice(h0 * HEAD_DIM, (h0 + 2) * HEAD_DIM)
            xs_pair = xc_ref[rows, pcols].astype(BF16)
            xs_halves = (jnp.where(lane_lo, xs_pair, jnp.zeros_like(xs_pair)),
                         jnp.where(lane_lo, jnp.zeros_like(xs_pair), xs_pair))
            y_pair = None
            s_pair = None
            a_b = []
            for k in range(2):
                h = h0 + k
                ab = jnp.broadcast_to(acum[:, h:h + 1], (CHUNK, CHUNK))
                a_b.append(ab)
                dec = jnp.exp(jnp.where(causal, ab - acum_t[h:h + 1, :], NEG_BIG))
                wmat = (cb * dec * dt_t[h:h + 1, :]).astype(BF16)
                yd = _dot(wmat, xs_halves[k])
                y_pair = yd if y_pair is None else y_pair + yd
                su = _dot((b_t * w_t[h:h + 1, :]).astype(BF16), xs_halves[k])
                s_pair = su if s_pair is None else s_pair + su
            ea_pair = jnp.exp(jnp.where(lane_lo, a_b[0], a_b[1]))
            lo = (2 * pr) * HEAD_DIM
            y_scr[:, pcols] = y_pair + y_off[:, lo:lo + 2 * HEAD_DIM] * ea_pair
            st_ref[:, pcols] = st_ref[:, pcols] * cdec_e[:, pcols] + s_pair

    gw = D_INNER // N_GROUPS
    for g in range(N_GROUPS):
        cols = slice(g * gw, (g + 1) * gw)
        y = y_scr[:, cols] + dskip_row[:, cols] * xc_ref[rows, cols]
        y = y * _silu(z_ref[slot, :, cols])
        y = _rmsnorm_rows(y, ng_row[:, cols])
        y_ref[0, out_rows, cols] = y.astype(y_ref.dtype)


def _ssd_prompt_kernel(x_ref, g1_ref, w_ref, cw_ref, cb_ref, dtb_ref, alog_ref, dskip_ref, ng_ref,
                       e_ref, y_ref, st_out_ref, conv_out_ref,
                       ext_ref, xc_ref, z_ref, dt_ref, st_ref, y_scr, *, tl):
    j = pl.program_id(1)
    nj = pl.num_programs(1)
    nchunk = tl // CHUNK
    hist = 8

    @pl.when(j == 0)
    def _():
        st_ref[...] = jnp.zeros_like(st_ref)
        ext_ref[0, 0:hist, :] = jnp.zeros((hist, CONV_DIM), F32)

    dtb_row = dtb_ref[...]
    a_row = -jnp.exp(alog_ref[...])
    dskip_row = dskip_ref[...]
    ng_row = ng_ref[...]

    def project(rows, slot):
        h = _rmsnorm_rows(x_ref[0, rows, :], g1_ref[...]).astype(BF16)
        z_ref[slot] = _dot(h, w_ref[:, 0:D_INNER])
        ext_ref[slot, hist:hist + CHUNK, :] = _dot(h, w_ref[:, D_INNER:D_INNER + CONV_DIM])
        dt_ref[slot] = _dot(h, w_ref[:, D_INNER + CONV_DIM:W_SSD_COLS])

    def process(rows, slot):
        ext_ref[1 - slot, 0:hist, :] = ext_ref[slot, CHUNK:CHUNK + hist, :]
        cblk = 512
        for cbk in range(CONV_DIM // cblk):
            cols = slice(cbk * cblk, (cbk + 1) * cblk)
            win = ext_ref[slot, :, cols]
            acc = cb_ref[:, cols] + cw_ref[CONV_W - 1:CONV_W, cols] * win[hist:, :]
            for k in range(1, CONV_W):
                tap = cw_ref[CONV_W - 1 - k:CONV_W - k, cols]
                acc = acc + tap * pltpu.roll(win, k, axis=0)[hist:, :]
            xc_ref[:, cols] = _silu(acc)
        _ssd_chunk(rows, slot, xc_ref, dt_ref, z_ref, st_ref, y_scr, y_ref, e_ref,
                   dtb_row, a_row, dskip_row, ng_row)

    project(pl.ds(0, CHUNK), 0)

    def body(i, carry):
        slot = i % 2
        process(pl.ds(pl.multiple_of(i * CHUNK, CHUNK), CHUNK), slot)
        project(pl.ds(pl.multiple_of((i + 1) * CHUNK, CHUNK), CHUNK), 1 - slot)
        return carry

    lax.fori_loop(0, nchunk - 1, body, 0)
    last = (nchunk - 1) % 2
    process(pl.ds((nchunk - 1) * CHUNK, CHUNK), last)

    @pl.when(j == nj - 1)
    def _():
        conv_out_ref[0] = ext_ref[last, hist + CHUNK - (CONV_W - 1):hist + CHUNK, :]
        for k in range(D_INNER // D_STATE):
            st_out_ref[0, k * D_STATE:(k + 1) * D_STATE, :] = st_ref[:, k * D_STATE:(k + 1) * D_STATE].T


def _ssd_prompt(x, g1, w_ssd, conv_w, conv_b, dtb, alog, dskip_e, ng, expand, *, tl=512):
    n, s, _ = x.shape
    kern = functools.partial(_ssd_prompt_kernel, tl=tl)
    const = lambda b, j: (0, 0)
    return pl.pallas_call(
        kern,
        grid=(n, s // tl),
        in_specs=[
            pl.BlockSpec((1, tl, D_MODEL), lambda b, j: (b, j, 0)),
            pl.BlockSpec((1, D_MODEL), const),
            pl.BlockSpec((D_MODEL, W_SSD_COLS), const),
            pl.BlockSpec((CONV_W, CONV_DIM), const),
            pl.BlockSpec((1, CONV_DIM), const),
            pl.BlockSpec((1, DT_PAD), const),
            pl.BlockSpec((1, DT_PAD), const),
            pl.BlockSpec((1, D_INNER), const),
            pl.BlockSpec((1, D_INNER), const),
            pl.BlockSpec((DT_PAD, D_INNER), const),
        ],
        out_specs=[
            pl.BlockSpec((1, tl, D_INNER), lambda b, j: (b, j, 0)),
            pl.BlockSpec((1, D_INNER, D_STATE), lambda b, j: (b, 0, 0)),
            pl.BlockSpec((1, CONV_W - 1, CONV_DIM), lambda b, j: (b, 0, 0)),
        ],
        out_shape=[
            jax.ShapeDtypeStruct((n, s, D_INNER), BF16),
            jax.ShapeDtypeStruct((n, D_INNER, D_STATE), F32),
            jax.ShapeDtypeStruct((n, CONV_W - 1, CONV_DIM), F32),
        ],
        scratch_shapes=[
            pltpu.VMEM((2, CHUNK + 8, CONV_DIM), F32),
            pltpu.VMEM((CHUNK, CONV_DIM), F32),
            pltpu.VMEM((2, CHUNK, D_INNER), F32),
            pltpu.VMEM((2, CHUNK, DT_PAD), F32),
            pltpu.VMEM((D_STATE, D_INNER), F32),
            pltpu.VMEM((CHUNK, D_INNER), F32),
        ],
        compiler_params=pltpu.CompilerParams(
            dimension_semantics=("parallel", "arbitrary"), vmem_limit_bytes=VMEM_LIMIT),
        name="ssd_prompt",
    )(x, g1, w_ssd, conv_w, conv_b, dtb, alog, dskip_e, ng, expand)


def _prep_ssd_params(norm1_g, w_in, conv_w, conv_b, dt_bias, a_log, d_skip, ssm_norm_g):
    w_ssd = jnp.concatenate(
        [w_in[:, 0:OFF_DT], w_in[:, OFF_DT:OFF_Q], jnp.zeros((D_MODEL, DT_PAD - N_HEADS), w_in.dtype)],
        axis=1).astype(BF16)
    pad = lambda v: jnp.pad(v, (0, DT_PAD - N_HEADS)).reshape(1, DT_PAD)
    expand = (jnp.arange(DT_PAD)[:, None] == (jnp.arange(D_INNER) // HEAD_DIM)[None, :]).astype(BF16)
    return dict(
        g1=norm1_g.reshape(1, D_MODEL), w_ssd=w_ssd, conv_w=conv_w, conv_b=conv_b.reshape(1, CONV_DIM),
        dtb=pad(dt_bias), alog=pad(a_log), dskip_e=jnp.repeat(d_skip, HEAD_DIM).reshape(1, D_INNER),
        ng=ssm_norm_g.reshape(1, D_INNER), expand=expand)


QKV_COLS = 9 * ATTN_WIDTH
KV_WIDTH = 2 * ATTN_WIDTH
LANES = 128
ATTN_TILE = 512
ATTN_SUPER = WINDOWS[2]
SUB_STEPS = ATTN_SUPER // ATTN_TILE


def _alibi_slope(group, head):
    return 2.0 ** (-8.0 * (group * ATTN_HEADS + head + 1) / (3 * ATTN_HEADS))


def _banded_block(group, q, kv, ok):
    dil = DILATIONS[group]
    blk = ATTN_BLOCK
    delta = (lax.broadcasted_iota(jnp.int32, (blk, 2 * blk), 0) + blk
             - lax.broadcasted_iota(jnp.int32, (blk, 2 * blk), 1))
    dist = (dil * delta).astype(F32)
    lane_lo_q = lax.broadcasted_iota(jnp.int32, (blk, 2 * ATTN_DIM), 1) < ATTN_DIM
    lane_lo_v = lax.broadcasted_iota(jnp.int32, (2 * blk, 2 * ATTN_DIM), 1) < ATTN_DIM
    out = []
    for pr in range(ATTN_HEADS // 2):
        cols = slice(pr * 2 * ATTN_DIM, (pr + 1) * 2 * ATTN_DIM)
        vcols = slice(ATTN_WIDTH + pr * 2 * ATTN_DIM, ATTN_WIDTH + (pr + 1) * 2 * ATTN_DIM)
        q_pair, k_pair, v_pair = q[:, cols], kv[:, cols], kv[:, vcols]
        o_pair = None
        lse = []
        for k in range(2):
            sel_q = lane_lo_q if k == 0 else ~lane_lo_q
            sel_v = lane_lo_v if k == 0 else ~lane_lo_v
            q_h = jnp.where(sel_q, q_pair, jnp.zeros_like(q_pair))
            v_h = jnp.where(sel_v, v_pair, jnp.zeros_like(v_pair))
            s = _dot_nt(q_h, k_pair)
            s = jnp.where(ok, s - _alibi_slope(group, 2 * pr + k) * dist, NEG_BIG)
            m = jnp.max(s, axis=-1, keepdims=True)
            p = jnp.exp(s - m)
            den = jnp.sum(p, axis=-1, keepdims=True)
            o_h = _dot(p.astype(BF16), v_h) / den
            o_pair = o_h if o_pair is None else o_pair + o_h
            lse.append(m + jnp.log(den))
        out.append((o_pair, jnp.where(lane_lo_q, lse[0], lse[1])))
    return out


def _attn_prompt_kernel(x_ref, g1_ref, w_ref, o_ref, t0_ref, t1_ref, t2_ref,
                        nat, q0b, kv0b, q1b, kv1b, q2b, kv2b, o_nat, l_nat):
    j = pl.program_id(1)
    nj = pl.num_programs(1)
    p = j % SUB_STEPS
    blk = ATTN_BLOCK
    tm = ATTN_TILE
    qpan = ATTN_WIDTH // LANES
    kvpan = KV_WIDTH // LANES

    @pl.when(j == 0)
    def _():
        kv0b[0:blk, :] = jnp.zeros((blk, KV_WIDTH), BF16)
        kv1b[:, 0:blk, :] = jnp.zeros((DILATIONS[1], blk, KV_WIDTH), BF16)
        kv2b[:, 0:blk, :] = jnp.zeros((DILATIONS[2], blk, KV_WIDTH), BF16)

    @pl.when(j > 0)
    def _():
        kv0b[0:blk, :] = kv0b[tm:tm + blk, :]
        kv1b[:, 0:blk, :] = kv1b[:, blk:2 * blk, :]

    @pl.when((j > 0) & (p == 0))
    def _():
        kv2b[:, 0:blk, :] = kv2b[:, blk:2 * blk, :]

    h = _rmsnorm_rows(x_ref[0], g1_ref[...]).astype(BF16)
    for c in range(QKV_COLS // LANES):
        res = _dot(h, w_ref[:, c * LANES:(c + 1) * LANES])
        nat[c] = res * (ATTN_DIM ** -0.5) if c < 3 * qpan else res

    for g, t_ref in enumerate((t0_ref, t1_ref, t2_ref)):
        keep = WINDOWS[g]
        kv_first = 3 * qpan + g * kvpan
        if keep >= tm:
            @pl.when(j >= nj - keep // tm)
            def _():
                for c in range(kvpan):
                    t_ref[0, :, c * LANES:(c + 1) * LANES] = nat[kv_first + c]
        else:
            @pl.when(j == nj - 1)
            def _():
                for c in range(kvpan):
                    t_ref[0, :, c * LANES:(c + 1) * LANES] = nat[kv_first + c, tm - keep:tm, :]

    for c in range(qpan):
        q0b[:, c * LANES:(c + 1) * LANES] = nat[c].astype(BF16)
    for c in range(kvpan):
        kv0b[blk:blk + tm, c * LANES:(c + 1) * LANES] = nat[3 * qpan + c].astype(BF16)
    d1 = DILATIONS[1]
    for r in range(d1):
        for c in range(qpan):
            q1b[r, :, c * LANES:(c + 1) * LANES] = nat[qpan + c, pl.ds(r, tm // d1, stride=d1), :].astype(BF16)
        for c in range(kvpan):
            kv1b[r, blk:2 * blk, c * LANES:(c + 1) * LANES] = (
                nat[3 * qpan + kvpan + c, pl.ds(r, tm // d1, stride=d1), :].astype(BF16))
    d2 = DILATIONS[2]
    sub = tm // d2
    row2 = pl.multiple_of(p * sub, sub)
    for r in range(d2):
        for c in range(qpan):
            q2b[r, pl.ds(row2, sub), c * LANES:(c + 1) * LANES] = (
                nat[2 * qpan + c, pl.ds(r, sub, stride=d2), :].astype(BF16))
        for c in range(kvpan):
            kv2b[r, pl.ds(blk + row2, sub), c * LANES:(c + 1) * LANES] = (
                nat[3 * qpan + 2 * kvpan + c, pl.ds(r, sub, stride=d2), :].astype(BF16))

    key_idx = lax.broadcasted_iota(jnp.int32, (blk, 2 * blk), 1)
    delta = lax.broadcasted_iota(jnp.int32, (blk, 2 * blk), 0) + blk - key_idx
    in_band = (delta >= 0) & (delta <= blk)
    own_key = key_idx >= blk
    base = pl.multiple_of(p * tm, tm)

    def store(group, rows, res):
        for pr, (o_pair, l_pair) in enumerate(res):
            o_nat[group, pr, rows, :] = o_pair
            l_nat[group, pr, rows, :] = l_pair

    for i in range(tm // blk):
        ok = in_band & (own_key | (j > 0)) if i == 0 else in_band
        res = _banded_block(0, q0b[i * blk:(i + 1) * blk, :], kv0b[i * blk:(i + 2) * blk, :], ok)
        store(0, pl.ds(base + i * blk, blk), res)

    ok1 = in_band & (own_key | (j > 0))
    for r in range(d1):
        store(1, pl.ds(base + r, blk, stride=d1), _banded_block(1, q1b[r], kv1b[r], ok1))

    @pl.when(p == SUB_STEPS - 1)
    def _():
        ok2 = in_band & (own_key | (j >= SUB_STEPS))
        unroll = 4

        def g2_body(it, carry):
            for k in range(unroll):
                r = it * unroll + k
                store(2, pl.ds(r, blk, stride=d2), _banded_block(2, q2b[r], kv2b[r], ok2))
            return carry

        lax.fori_loop(0, d2 // unroll, g2_body, 0)

        rb = 256
        for k in range(ATTN_SUPER // rb):
            rows = slice(k * rb, (k + 1) * rb)
            for pr in range(ATTN_HEADS // 2):
                l0, l1, l2 = l_nat[0, pr, rows, :], l_nat[1, pr, rows, :], l_nat[2, pr, rows, :]
                m = jnp.maximum(jnp.maximum(l0, l1), l2)
                e0, e1, e2 = jnp.exp(l0 - m), jnp.exp(l1 - m), jnp.exp(l2 - m)
                o = (e0 * o_nat[0, pr, rows, :] + e1 * o_nat[1, pr, rows, :]
                     + e2 * o_nat[2, pr, rows, :]) / (e0 + e1 + e2)
                o_ref[0, rows, pr * LANES:(pr + 1) * LANES] = o.astype(o_ref.dtype)


def _attn_prompt(x, g1, w_qkv):
    n, s, _ = x.shape
    tm = ATTN_TILE
    nj = s // tm
    assert s % ATTN_SUPER == 0
    const = lambda b, j: (0, 0)

    def tail_spec(keep):
        if keep >= tm:
            first = nj - keep // tm
            return pl.BlockSpec((1, tm, KV_WIDTH), lambda b, j: (b, jnp.maximum(j - first, 0), 0))
        return pl.BlockSpec((1, keep, KV_WIDTH), lambda b, j: (b, 0, 0))

    d1, d2 = DILATIONS[1], DILATIONS[2]
    pairs = ATTN_HEADS // 2
    return pl.pallas_call(
        _attn_prompt_kernel,
        grid=(n, nj),
        in_specs=[pl.BlockSpec((1, tm, D_MODEL), lambda b, j: (b, j, 0)), pl.BlockSpec((1, D_MODEL), const),
                  pl.BlockSpec((D_MODEL, QKV_COLS), const)],
        out_specs=[pl.BlockSpec((1, ATTN_SUPER, ATTN_WIDTH), lambda b, j: (b, j // SUB_STEPS, 0))]
        + [tail_spec(w) for w in WINDOWS],
        out_shape=[jax.ShapeDtypeStruct((n, s, ATTN_WIDTH), BF16)]
        + [jax.ShapeDtypeStruct((n, w, KV_WIDTH), F32) for w in WINDOWS],
        scratch_shapes=[
            pltpu.VMEM((QKV_COLS // LANES, tm, LANES), F32),
            pltpu.VMEM((tm, ATTN_WIDTH), BF16),
            pltpu.VMEM((ATTN_BLOCK + tm, KV_WIDTH), BF16),
            pltpu.VMEM((d1, tm // d1, ATTN_WIDTH), BF16),
            pltpu.VMEM((d1, ATTN_BLOCK + tm // d1, KV_WIDTH), BF16),
            pltpu.VMEM((d2, ATTN_SUPER // d2, ATTN_WIDTH), BF16),
            pltpu.VMEM((d2, ATTN_BLOCK + ATTN_SUPER // d2, KV_WIDTH), BF16),
            pltpu.VMEM((3, pairs, ATTN_SUPER, LANES), F32),
            pltpu.VMEM((3, pairs, ATTN_SUPER, LANES), F32),
        ],
        compiler_params=pltpu.CompilerParams(
            dimension_semantics=("parallel", "arbitrary"), vmem_limit_bytes=VMEM_LIMIT),
        name="attn_prompt",
    )(x, g1, w_qkv)


def _merge_kernel(x_ref, ys_ref, oa_ref, g1_ref, wg_ref, ws_ref, wa_ref, wo_ref, out_ref):
    xf = x_ref[...]
    h = _rmsnorm_rows(xf, g1_ref[...]).astype(BF16)
    gates = _dot(h, wg_ref[...])
    gates = 1.0 / (1.0 + jnp.exp(-gates))
    merged = (gates[:, 0:D_MODEL] * _dot(ys_ref[...], ws_ref[...])
              + gates[:, D_MODEL:2 * D_MODEL] * _dot(oa_ref[...].astype(BF16), wa_ref[...]))
    out_ref[...] = xf + _dot(merged.astype(BF16), wo_ref[...])


def _merge(x2d, y_ssm, o_attn, g1, w_gates, w_ssm_out, w_attn_out, w_o, *, tm):
    m = x2d.shape[0]
    row = lambda i: (i, 0)
    const = lambda i: (0, 0)
    return pl.pallas_call(
        _merge_kernel,
        grid=(m // tm,),
        in_specs=[pl.BlockSpec((tm, D_MODEL), row), pl.BlockSpec((tm, D_INNER), row),
                  pl.BlockSpec((tm, ATTN_WIDTH), row),
                  pl.BlockSpec((1, D_MODEL), const), pl.BlockSpec((D_MODEL, 2 * D_MODEL), const),
                  pl.BlockSpec((D_INNER, D_MODEL), const), pl.BlockSpec((ATTN_WIDTH, D_MODEL), const),
                  pl.BlockSpec((D_MODEL, D_MODEL), const)],
        out_specs=pl.BlockSpec((tm, D_MODEL), row),
        out_shape=jax.ShapeDtypeStruct((m, D_MODEL), F32),
        compiler_params=pltpu.CompilerParams(
            dimension_semantics=("parallel",), vmem_limit_bytes=VMEM_LIMIT),
        name="merge",
    )(x2d, y_ssm, o_attn, g1, w_gates, w_ssm_out, w_attn_out, w_o)


def _ffn_kernel(x_ref, g2_ref, gf_ref, wg_ref, wu_ref, wd_ref, out_ref, h_scr, acc_scr):
    f = pl.program_id(1)

    @pl.when(f == 0)
    def _():
        h_scr[...] = _rmsnorm_rows(x_ref[...], g2_ref[...]).astype(BF16)
        acc_scr[...] = jnp.zeros_like(acc_scr)

    h = h_scr[...]
    gate = _dot(h, wg_ref[...])
    act = gate * (1.0 / (1.0 + jnp.exp(-gate))) * _dot(h, wu_ref[...])
    acc_scr[...] += _dot(act.astype(BF16), wd_ref[...])

    @pl.when(f == pl.num_programs(1) - 1)
    def _():
        out_ref[...] = _rmsnorm_rows(x_ref[...] + acc_scr[...], gf_ref[...])


def _ffn(x2d, g2, gf, w_gate, w_up, w_down, *, tm, tf=256):
    m = x2d.shape[0]
    row = lambda i, f: (i, 0)
    const = lambda i, f: (0, 0)
    return pl.pallas_call(
        _ffn_kernel,
        grid=(m // tm, D_FF // tf),
        in_specs=[pl.BlockSpec((tm, D_MODEL), row), pl.BlockSpec((1, D_MODEL), const),
                  pl.BlockSpec((1, D_MODEL), const),
                  pl.BlockSpec((D_MODEL, tf), lambda i, f: (0, f)),
                  pl.BlockSpec((D_MODEL, tf), lambda i, f: (0, f)),
                  pl.BlockSpec((tf, D_MODEL), lambda i, f: (f, 0))],
        out_specs=pl.BlockSpec((tm, D_MODEL), row),
        out_shape=jax.ShapeDtypeStruct((m, D_MODEL), F32),
        scratch_shapes=[pltpu.VMEM((tm, D_MODEL), BF16), pltpu.VMEM((tm, D_MODEL), F32)],
        compiler_params=pltpu.CompilerParams(
            dimension_semantics=("parallel", "arbitrary"), vmem_limit_bytes=VMEM_LIMIT),
        name="ffn",
    )(x2d, g2, gf, w_gate, w_up, w_down)


SAMPLE_BLOCK = 8


def _ssd_sample_pre_kernel(x_ref, g1_ref, w_ref, cs_ref, cw_ref, cb_ref, dtb_ref, alog_ref, e_ref,
                           z_ref, xs_ref, b_ref, c_ref, xdt_t_ref, da_t_ref, conv_out_ref):
    h = _rmsnorm_rows(x_ref[...], g1_ref[...]).astype(BF16)
    z_ref[...] = _dot(h, w_ref[:, 0:D_INNER])
    xbc = _dot(h, w_ref[:, D_INNER:D_INNER + CONV_DIM])
    dt_raw = _dot(h, w_ref[:, D_INNER + CONV_DIM:W_SSD_COLS])
    acc = cb_ref[...] + cw_ref[CONV_W - 1:CONV_W, :] * xbc
    for k in range(CONV_W - 1):
        hist = cs_ref[k]
        acc = acc + cw_ref[k:k + 1, :] * hist
        if k > 0:
            conv_out_ref[k - 1] = hist
    conv_out_ref[CONV_W - 2] = xbc
    xc = _silu(acc)
    xs = xc[:, 0:D_INNER]
    xs_ref[...] = xs
    b_ref[...] = xc[:, D_INNER:D_INNER + N_GROUPS * D_STATE]
    c_ref[...] = xc[:, D_INNER + N_GROUPS * D_STATE:CONV_DIM]
    dt = _softplus(dt_raw + dtb_ref[...])
    da = jnp.exp(dt * (-jnp.exp(alog_ref[...])))
    xdt = xs * _dot_f32_lhs(dt, e_ref[...])
    da_e = _dot_f32_lhs(da, e_ref[...])
    nb = x_ref.shape[0]
    for k in range(D_INNER // nb):
        xdt_t_ref[k * nb:(k + 1) * nb, :] = xdt[:, k * nb:(k + 1) * nb].T
        da_t_ref[k * nb:(k + 1) * nb, :] = da_e[:, k * nb:(k + 1) * nb].T


def _ssd_sample_pre(x2d, conv_state_t, p):
    nb = x2d.shape[0]
    full = lambda shape: pl.BlockSpec(shape, lambda i: (0,) * len(shape))
    return pl.pallas_call(
        _ssd_sample_pre_kernel,
        grid=(1,),
        in_specs=[full((nb, D_MODEL)), full((1, D_MODEL)), full((D_MODEL, W_SSD_COLS)),
                  full((CONV_W - 1, nb, CONV_DIM)), full((CONV_W, CONV_DIM)), full((1, CONV_DIM)),
                  full((1, DT_PAD)), full((1, DT_PAD)), full((DT_PAD, D_INNER))],
        out_specs=[full((nb, D_INNER)), full((nb, D_INNER)), full((nb, N_GROUPS * D_STATE)),
                   full((nb, N_GROUPS * D_STATE)), full((D_INNER, nb)), full((D_INNER, nb)),
                   full((CONV_W - 1, nb, CONV_DIM))],
        out_shape=[jax.ShapeDtypeStruct((nb, D_INNER), F32), jax.ShapeDtypeStruct((nb, D_INNER), F32),
                   jax.ShapeDtypeStruct((nb, N_GROUPS * D_STATE), F32),
                   jax.ShapeDtypeStruct((nb, N_GROUPS * D_STATE), F32),
                   jax.ShapeDtypeStruct((D_INNER, nb), F32), jax.ShapeDtypeStruct((D_INNER, nb), F32),
                   jax.ShapeDtypeStruct((CONV_W - 1, nb, CONV_DIM), F32)],
        compiler_params=pltpu.CompilerParams(vmem_limit_bytes=VMEM_LIMIT),
        name="ssd_sample_pre",
    )(x2d, p["g1"], p["w_ssd"], conv_state_t, p["conv_w"], p["conv_b"], p["dtb"], p["alog"], p["expand"])


def _ssd_sample_state_kernel(st_ref, xdt_ref, da_ref, b_ref, c_ref, xs_ref, z_ref, dskip_ref, ng_ref,
                             st_out_ref, y_ref, y_scr):
    grows = D_INNER // N_GROUPS
    for i in range(SAMPLE_BLOCK):
        for g in range(N_GROUPS):
            rows = slice(g * grows, (g + 1) * grows)
            s_new = (st_ref[i, rows, :] * da_ref[0, rows, i:i + 1]
                     + xdt_ref[0, rows, i:i + 1] * b_ref[i, g:g + 1, :])
            st_out_ref[i, rows, :] = s_new
            y_t = _dot_nt(c_ref[i].astype(BF16), s_new.astype(BF16))
            y_scr[i:i + 1, rows] = y_t[g:g + 1, :]
    for g in range(N_GROUPS):
        cols = slice(g * grows, (g + 1) * grows)
        y = y_scr[:, cols] + dskip_ref[:, cols] * xs_ref[:, cols]
        y = y * _silu(z_ref[:, cols])
        y_ref[:, cols] = _rmsnorm_rows(y, ng_ref[:, cols]).astype(y_ref.dtype)


def _ssd_sample_state(state, xdt3, da3, b3, c3, xs, z, p):
    nb = state.shape[0]
    bb = SAMPLE_BLOCK
    const = lambda i: (0, 0)
    return pl.pallas_call(
        _ssd_sample_state_kernel,
        grid=(nb // bb,),
        in_specs=[pl.BlockSpec((bb, D_INNER, D_STATE), lambda i: (i, 0, 0)),
                  pl.BlockSpec((1, D_INNER, bb), lambda i: (i, 0, 0)),
                  pl.BlockSpec((1, D_INNER, bb), lambda i: (i, 0, 0)),
                  pl.BlockSpec((bb, N_GROUPS, D_STATE), lambda i: (i, 0, 0)),
                  pl.BlockSpec((bb, N_GROUPS, D_STATE), lambda i: (i, 0, 0)),
                  pl.BlockSpec((bb, D_INNER), lambda i: (i, 0)),
                  pl.BlockSpec((bb, D_INNER), lambda i: (i, 0)),
                  pl.BlockSpec((1, D_INNER), const), pl.BlockSpec((1, D_INNER), const)],
        out_specs=[pl.BlockSpec((bb, D_INNER, D_STATE), lambda i: (i, 0, 0)),
                   pl.BlockSpec((bb, D_INNER), lambda i: (i, 0))],
        out_shape=[jax.ShapeDtypeStruct((nb, D_INNER, D_STATE), F32),
                   jax.ShapeDtypeStruct((nb, D_INNER), BF16)],
        scratch_shapes=[pltpu.VMEM((bb, D_INNER), F32)],
        compiler_params=pltpu.CompilerParams(
            dimension_semantics=("parallel",), vmem_limit_bytes=VMEM_LIMIT),
        name="ssd_sample_state",
    )(state, xdt3, da3, b3, c3, xs, z, p["dskip_e"], p["ng"])


ATTN_SAMPLE_BLOCK = 2


def _proj_rows_kernel(x_ref, g1_ref, w_ref, wt_ref, out_ref, out_t_ref):
    h = _rmsnorm_rows(x_ref[...], g1_ref[...]).astype(BF16)
    out_ref[...] = _dot(h, w_ref[...])
    out_t_ref[...] = _dot_nt(wt_ref[...], h)


def _proj_rows(x2d, g1, w, w_t):
    nb, cols = x2d.shape[0], w.shape[1]
    full = lambda shape: pl.BlockSpec(shape, lambda i: (0,) * len(shape))
    return pl.pallas_call(
        _proj_rows_kernel,
        grid=(1,),
        in_specs=[full((nb, D_MODEL)), full((1, D_MODEL)), full((D_MODEL, cols)), full((cols, D_MODEL))],
        out_specs=[full((nb, cols)), full((cols, nb))],
        out_shape=[jax.ShapeDtypeStruct((nb, cols), F32), jax.ShapeDtypeStruct((cols, nb), F32)],
        compiler_params=pltpu.CompilerParams(vmem_limit_bytes=VMEM_LIMIT),
        name="proj_rows",
    )(x2d, g1, w, w_t)


def _attn_sample_kernel(qkv_t_ref, c0_ref, c1_ref, c2_ref, o_t_ref):
    kv0 = 3 * ATTN_WIDTH
    for i in range(ATTN_SAMPLE_BLOCK):
        lane = slice(i, i + 1)
        for h in range(ATTN_HEADS):
            hrows = slice(h * ATTN_DIM, (h + 1) * ATTN_DIM)
            scores, new_scores = [], []
            for g, c_ref in enumerate((c0_ref, c1_ref, c2_ref)):
                w, dil = WINDOWS[g], DILATIONS[g]
                q_col = qkv_t_ref[0, g * ATTN_WIDTH + h * ATTN_DIM:g * ATTN_WIDTH + (h + 1) * ATTN_DIM, lane]
                q_col = q_col * (ATTN_DIM ** -0.5)
                k_lo = kv0 + g * KV_WIDTH + h * ATTN_DIM
                k_col = qkv_t_ref[0, k_lo:k_lo + ATTN_DIM, lane]
                s = jnp.sum(c_ref[i, hrows, :] * q_col, axis=0, keepdims=True)
                pos = lax.broadcasted_iota(jnp.int32, (1, w), 1)
                on_stride = (pos & (dil - 1)) == 0
                s = jnp.where(on_stride, s - _alibi_slope(g, h) * (w - pos).astype(F32), NEG_BIG)
                scores.append(s)
                new_scores.append(jnp.sum(q_col * k_col, axis=0, keepdims=True))
            m = new_scores[0]
            for g in range(3):
                m = jnp.maximum(m, jnp.maximum(new_scores[g], jnp.max(scores[g], axis=-1, keepdims=True)))
            den = jnp.zeros((1, 1), F32)
            acc = jnp.zeros((ATTN_DIM, 1), F32)
            for g, c_ref in enumerate((c0_ref, c1_ref, c2_ref)):
                p = jnp.exp(scores[g] - m)
                p_new = jnp.exp(new_scores[g] - m)
                den = den + jnp.sum(p, axis=-1, keepdims=True) + p_new
                v_lo = kv0 + g * KV_WIDTH + ATTN_WIDTH + h * ATTN_DIM
                v_col = qkv_t_ref[0, v_lo:v_lo + ATTN_DIM, lane]
                v_t = c_ref[i, ATTN_WIDTH + h * ATTN_DIM:ATTN_WIDTH + (h + 1) * ATTN_DIM, :]
                acc = acc + jnp.sum(v_t * p, axis=-1, keepdims=True) + p_new * v_col
            o_t_ref[0, hrows, lane] = acc / den


def _attn_sample(qkv_t, caches):
    nb = qkv_t.shape[1]
    bb = ATTN_SAMPLE_BLOCK
    views = []
    for g, c in enumerate(caches):
        assert c.shape[1] == WINDOWS[g]
        views.append(jnp.transpose(c, (0, 2, 3, 4, 1)).reshape(nb, KV_WIDTH, WINDOWS[g]))
    qkv_blocks = qkv_t.reshape(QKV_COLS, nb // bb, bb).transpose(1, 0, 2)
    o_t = pl.pallas_call(
        _attn_sample_kernel,
        grid=(nb // bb,),
        in_specs=[pl.BlockSpec((1, QKV_COLS, bb), lambda i: (i, 0, 0))]
        + [pl.BlockSpec((bb, KV_WIDTH, w), lambda i: (i, 0, 0)) for w in WINDOWS],
        out_specs=pl.BlockSpec((1, ATTN_WIDTH, bb), lambda i: (i, 0, 0)),
        out_shape=jax.ShapeDtypeStruct((nb // bb, ATTN_WIDTH, bb), F32),
        compiler_params=pltpu.CompilerParams(
            dimension_semantics=("parallel",), vmem_limit_bytes=VMEM_LIMIT),
        name="attn_sample",
    )(qkv_blocks, *views)
    return o_t.transpose(0, 2, 1).reshape(nb, ATTN_WIDTH)


def _layer_sample(x2d, state_ssm, state_conv, caches, wp):
    nb = x2d.shape[0]
    p = wp["ssd"]
    bb = SAMPLE_BLOCK
    z, xs, bm, cm, xdt_t, da_t, conv_new = _ssd_sample_pre(
        x2d, jnp.transpose(state_conv, (1, 0, 2)), p)
    to_blocks = lambda t: t.reshape(D_INNER, nb // bb, bb).transpose(1, 0, 2)
    st_new, y_ssm = _ssd_sample_state(
        state_ssm.reshape(nb, D_INNER, D_STATE), to_blocks(xdt_t), to_blocks(da_t),
        bm.reshape(nb, N_GROUPS, D_STATE), cm.reshape(nb, N_GROUPS, D_STATE), xs, z, p)
    qkv, qkv_t = _proj_rows(x2d, p["g1"], wp["w_qkv"], wp["w_qkv"].T)
    kv_new = qkv[:, 3 * ATTN_WIDTH:]
    o = _attn_sample(qkv_t, caches)
    x1 = _merge(x2d, y_ssm, o, p["g1"], wp["w_gates"], wp["w_ssm_out"], wp["w_attn_out"], wp["w_o"], tm=nb)
    y = _ffn(x1, wp["g2"], wp["gf"], wp["w_ffn_gate"], wp["w_ffn_up"], wp["w_ffn_down"], tm=nb)
    return y, st_new, conv_new, kv_new


def _prep_attn_weight(w_in):
    cols = [w_in[:, OFF_Q:OFF_Q + 3 * ATTN_WIDTH]]
    for g in range(3):
        cols.append(w_in[:, OFF_K + g * ATTN_WIDTH:OFF_K + (g + 1) * ATTN_WIDTH])
        cols.append(w_in[:, OFF_V + g * ATTN_WIDTH:OFF_V + (g + 1) * ATTN_WIDTH])
    return jnp.concatenate(cols, axis=1).astype(BF16)


def _layer_prompt(x, wp):
    n, s, _ = x.shape
    y_ssm, st, conv = _ssd_prompt(x, **wp["ssd"])
    o_attn, t0, t1, t2 = _attn_prompt(x, wp["ssd"]["g1"], wp["w_qkv"])
    x1 = _merge(x.reshape(n * s, D_MODEL), y_ssm.reshape(n * s, D_INNER), o_attn.reshape(n * s, ATTN_WIDTH),
                wp["ssd"]["g1"],
                wp["w_gates"], wp["w_ssm_out"], wp["w_attn_out"], wp["w_o"], tm=256)
    y = _ffn(x1, wp["g2"], wp["gf"], wp["w_ffn_gate"], wp["w_ffn_up"], wp["w_ffn_down"], tm=1024)
    return y.reshape(n, s, D_MODEL), st, conv, (t0, t1, t2)


def _prep_weights(norm1_g, w_in, conv_w, conv_b, dt_bias, a_log, d_skip, ssm_norm_g, w_ssm_out, w_attn_out,
                  w_o, norm2_g, w_ffn_gate, w_ffn_up, w_ffn_down, norm_f_g):
    return dict(
        ssd=_prep_ssd_params(norm1_g, w_in, conv_w, conv_b, dt_bias, a_log, d_skip, ssm_norm_g),
        w_qkv=_prep_attn_weight(w_in),
        w_gates=w_in[:, OFF_GS:OFF_GS + 2 * D_MODEL].astype(BF16),
        w_ssm_out=w_ssm_out.astype(BF16), w_attn_out=w_attn_out.astype(BF16), w_o=w_o.astype(BF16),
        g2=norm2_g.reshape(1, D_MODEL), gf=norm_f_g.reshape(1, D_MODEL),
        w_ffn_gate=w_ffn_gate.astype(BF16), w_ffn_up=w_ffn_up.astype(BF16),
        w_ffn_down=w_ffn_down.astype(BF16))


def kernel(x_prompt, x_sample, state_ssm, state_conv, cache_kv_w128, cache_kv_w512, cache_kv_w2048, norm1_g, w_in, conv_w, conv_b, dt_bias, a_log, d_skip, ssm_norm_g, w_ssm_out, w_attn_out, w_o, norm2_g, w_ffn_gate, w_ffn_up, w_ffn_down, norm_f_g):
    wp = _prep_weights(norm1_g[0], w_in[0], conv_w[0], conv_b[0], dt_bias[0], a_log[0], d_skip[0],
                       ssm_norm_g[0], w_ssm_out[0], w_attn_out[0], w_o[0], norm2_g[0], w_ffn_gate[0],
                       w_ffn_up[0], w_ffn_down[0], norm_f_g)
    n, s, _ = x_prompt.shape
    nb = x_sample.shape[0]
    y_p, st_p, conv_p, tails = _layer_prompt(x_prompt, wp)
    y_s, st_s, conv_s, kv_s = _layer_sample(
        x_sample.reshape(nb, D_MODEL), state_ssm[0], state_conv[0],
        (cache_kv_w128[0], cache_kv_w512[0], cache_kv_w2048[0]), wp)
    kv_tail_shape = (2, ATTN_HEADS, ATTN_DIM)
    outs = [y_p, y_s.reshape(nb, 1, D_MODEL),
            st_p.reshape(1, n, N_HEADS, HEAD_DIM, D_STATE), st_s.reshape(1, nb, N_HEADS, HEAD_DIM, D_STATE),
            conv_p.reshape(1, n, CONV_W - 1, CONV_DIM), jnp.transpose(conv_s, (1, 0, 2)).reshape(1, nb, CONV_W - 1, CONV_DIM)]
    for g in range(3):
        outs.append(tails[g].reshape((1, n, WINDOWS[g]) + kv_tail_shape))
        outs.append(kv_s[:, g * KV_WIDTH:(g + 1) * KV_WIDTH].reshape((1, nb, 1) + kv_tail_shape))
    return tuple(outs)
```

```python
import functools

import jax
import jax.numpy as jnp
from jax import lax
from jax.experimental import pallas as pl
from jax.experimental.pallas import tpu as pltpu

F32 = jnp.float32
BF16 = jnp.bfloat16

D_MODEL = 1024
D_INNER = 2048
HEAD_DIM = 64
N_HEADS = 32
N_GROUPS = 4
HEADS_PER_SSM_GROUP = N_HEADS // N_GROUPS
D_STATE = 128
CONV_W = 4
CONV_DIM = D_INNER + 2 * N_GROUPS * D_STATE
CHUNK = 128
DT_PAD = 128
ATTN_HEADS = 4
ATTN_DIM = 64
ATTN_WIDTH = ATTN_HEADS * ATTN_DIM
WINDOWS = (128, 512, 2048)
DILATIONS = (1, 4, 16)
ATTN_BLOCK = 128
D_FF = 2816
RMS_EPS = 1e-6
NEG_BIG = -1e30
OFF_XBC = D_INNER
OFF_DT = OFF_XBC + CONV_DIM
OFF_Q = OFF_DT + N_HEADS
OFF_K = OFF_Q + 3 * ATTN_WIDTH
OFF_V = OFF_K + 3 * ATTN_WIDTH
OFF_GS = OFF_V + 3 * ATTN_WIDTH
W_SSD_COLS = D_INNER + CONV_DIM + DT_PAD
VMEM_LIMIT = 52 * 1024 * 1024


def _dot(a, b):
    return jnp.dot(a, b, preferred_element_type=F32)


def _dot_nt(a, b):
    return lax.dot_general(a, b, (((1,), (1,)), ((), ())), preferred_element_type=F32)


def _split3(v):
    hi = v.astype(BF16)
    r1 = v - hi.astype(F32)
    mid = r1.astype(BF16)
    lo = (r1 - mid.astype(F32)).astype(BF16)
    return hi, mid, lo


def _dot_f32_rhs(a_bf16_exact, v):
    hi, mid, lo = _split3(v)
    return _dot(a_bf16_exact, hi) + _dot(a_bf16_exact, mid) + _dot(a_bf16_exact, lo)


def _dot_f32_lhs(v, b_bf16_exact):
    hi, mid, lo = _split3(v)
    return _dot(hi, b_bf16_exact) + _dot(mid, b_bf16_exact) + _dot(lo, b_bf16_exact)


def _rmsnorm_rows(xf, g_row):
    ms = jnp.mean(xf * xf, axis=-1, keepdims=True)
    return xf * lax.rsqrt(ms + RMS_EPS) * g_row


def _silu(v):
    h = 0.5 * v
    return h + h * jnp.tanh(h)


def _log1p(e):
    u = 1.0 + e
    return jnp.where(u == 1.0, e, jnp.log(u) * (e / (u - 1.0)))


def _softplus(v):
    return jnp.maximum(v, 0.0) + _log1p(jnp.exp(-jnp.abs(v)))


def _ssd_chunk(out_rows, slot, xc_ref, dt_ref, st_ref, y_ref, e_ref, dtb_row, a_row, dskip_row):
    rows = slice(None)
    tri_r = lax.broadcasted_iota(jnp.int32, (CHUNK, CHUNK), 0)
    tri_c = lax.broadcasted_iota(jnp.int32, (CHUNK, CHUNK), 1)
    causal = tri_r >= tri_c
    ltri = causal.astype(BF16)
    lane_lo = tri_c < HEAD_DIM

    dt = _softplus(dt_ref[slot] + dtb_row)
    adt = dt * a_row
    acum = _dot_f32_rhs(ltri, adt)
    acum_dt_t = (acum - jnp.log(dt)).T
    a_last = acum[CHUNK - 1:CHUNK, :]
    w_t = (dt * jnp.exp(a_last - acum)).T
    cdec = jnp.exp(a_last)
    cdec_e = _dot_f32_lhs(jnp.broadcast_to(cdec, (8, DT_PAD)), e_ref[...])[0:1, :]

    for g in range(N_GROUPS):
        b_g = xc_ref[rows, D_INNER + g * D_STATE:D_INNER + (g + 1) * D_STATE]
        c_g = xc_ref[rows, D_INNER + N_GROUPS * D_STATE + g * D_STATE:
                     D_INNER + N_GROUPS * D_STATE + (g + 1) * D_STATE]
        b_bf = b_g.astype(BF16)
        c_bf = c_g.astype(BF16)
        cb = _dot_nt(c_bf, b_bf)
        b_t = b_g.T
        gcols = slice(g * HEADS_PER_SSM_GROUP * HEAD_DIM, (g + 1) * HEADS_PER_SSM_GROUP * HEAD_DIM)
        y_off = _dot(c_bf, st_ref[:, gcols].astype(BF16))
        for pr in range(HEADS_PER_SSM_GROUP // 2):
            h0 = g * HEADS_PER_SSM_GROUP + 2 * pr
            pcols = slice(h0 * HEAD_DIM, (h0 + 2) * HEAD_DIM)
            xs_f32 = xc_ref[rows, pcols]
            xs_pair = xs_f32.astype(BF16)
            xs_diag = jnp.concatenate([jnp.where(lane_lo, xs_pair, jnp.zeros_like(xs_pair)),
                                       jnp.where(lane_lo, jnp.zeros_like(xs_pair), xs_pair)], axis=0)
            a_b, wmats, smats = [], [], []
            for k in range(2):
                h = h0 + k
                ab = jnp.broadcast_to(acum[:, h:h + 1], (CHUNK, CHUNK))
                a_b.append(ab)
                dec = jnp.exp(jnp.where(causal, ab - acum_dt_t[h:h + 1, :], NEG_BIG))
                wmats.append((cb * dec).astype(BF16))
                smats.append((b_t * w_t[h:h + 1, :]).astype(BF16))
            y_pair = _dot(jnp.concatenate(wmats, axis=1), xs_diag)
            s_pair = _dot(jnp.concatenate(smats, axis=1), xs_diag)
            ea_pair = jnp.exp(jnp.where(lane_lo, a_b[0], a_b[1]))
            lo = (2 * pr) * HEAD_DIM
            y_pre = y_pair + y_off[:, lo:lo + 2 * HEAD_DIM] * ea_pair + dskip_row[:, pcols] * xs_f32
            y_ref[0, out_rows, pcols] = y_pre.astype(y_ref.dtype)
            st_ref[:, pcols] = st_ref[:, pcols] * cdec_e[:, pcols] + s_pair


def _ssd_prompt_kernel(x_ref, g1_ref, w_ref, cw_ref, cb_ref, dtb_ref, alog_ref, dskip_ref,
                       e_ref, y_ref, z_out_ref, st_out_ref, conv_out_ref,
                       ext_ref, xc_ref, dt_ref, st_ref, h_scr, *, tl):
    j = pl.program_id(1)
    nj = pl.num_programs(1)
    nchunk = tl // CHUNK
    hist = 8

    @pl.when(j == 0)
    def _():
        st_ref[...] = jnp.zeros_like(st_ref)
        ext_ref[0, 0:hist, :] = jnp.zeros((hist, CONV_DIM), F32)

    dtb_row = dtb_ref[...]
    a_row = -jnp.exp(alog_ref[...])
    dskip_row = dskip_ref[...]

    piece = 512
    pieces = ([("z", lo, lo + piece) for lo in range(0, D_INNER, piece)]
              + [("xbc", lo, lo + piece) for lo in range(0, CONV_DIM, piece)] + [("dt", 0, DT_PAD)])
    cblk = 512
    n_conv = CONV_DIM // cblk

    def normalize(rows):
        h_scr[...] = _rmsnorm_rows(x_ref[0, rows, :], g1_ref[...]).astype(BF16)

    def project_piece(rows, slot, kind, lo, hi):
        if kind == "z":
            z_out_ref[0, rows, lo:hi] = _dot(h_scr[...], w_ref[:, lo:hi]).astype(z_out_ref.dtype)
        elif kind == "xbc":
            ext_ref[slot, hist:hist + CHUNK, lo:hi] = _dot(h_scr[...], w_ref[:, D_INNER + lo:D_INNER + hi])
        else:
            dt_ref[slot] = _dot(h_scr[...], w_ref[:, D_INNER + CONV_DIM:W_SSD_COLS])

    def conv_block(slot, cbk):
        cols = slice(cbk * cblk, (cbk + 1) * cblk)
        win = ext_ref[slot, :, cols]
        acc = cb_ref[:, cols] + cw_ref[CONV_W - 1:CONV_W, cols] * win[hist:, :]
        for k in range(1, CONV_W):
            tap = cw_ref[CONV_W - 1 - k:CONV_W - k, cols]
            acc = acc + tap * pltpu.roll(win, k, axis=0)[hist:, :]
        xc_ref[:, cols] = _silu(acc)

    def process(rows, slot, next_rows):
        ext_ref[1 - slot, 0:hist, :] = ext_ref[slot, CHUNK:CHUNK + hist, :]
        if next_rows is not None:
            normalize(next_rows)
        per_block = -(-len(pieces) // n_conv)
        for cbk in range(n_conv):
            conv_block(slot, cbk)
            if next_rows is not None:
                for kind, lo, hi in pieces[cbk * per_block:(cbk + 1) * per_block]:
                    project_piece(next_rows, 1 - slot, kind, lo, hi)
        _ssd_chunk(rows, slot, xc_ref, dt_ref, st_ref, y_ref, e_ref, dtb_row, a_row, dskip_row)

    first = pl.ds(0, CHUNK)
    normalize(first)
    for kind, lo, hi in pieces:
        project_piece(first, 0, kind, lo, hi)

    def body(i, carry):
        process(pl.ds(pl.multiple_of(i * CHUNK, CHUNK), CHUNK), i % 2,
                pl.ds(pl.multiple_of((i + 1) * CHUNK, CHUNK), CHUNK))
        return carry

    lax.fori_loop(0, nchunk - 1, body, 0)
    last = (nchunk - 1) % 2
    process(pl.ds((nchunk - 1) * CHUNK, CHUNK), last, None)

    @pl.when(j == nj - 1)
    def _():
        conv_out_ref[0] = ext_ref[last, hist + CHUNK - (CONV_W - 1):hist + CHUNK, :]
        for k in range(D_INNER // D_STATE):
            st_out_ref[0, k * D_STATE:(k + 1) * D_STATE, :] = st_ref[:, k * D_STATE:(k + 1) * D_STATE].T


def _ssd_prompt(x, g1, w_ssd, conv_w, conv_b, dtb, alog, dskip_e, expand, *, tl=512):
    n, s, _ = x.shape
    assert (tl // CHUNK) % 2 == 0 and s % tl == 0
    kern = functools.partial(_ssd_prompt_kernel, tl=tl)
    const = lambda b, j: (0, 0)
    return pl.pallas_call(
        kern,
        grid=(n, s // tl),
        in_specs=[
            pl.BlockSpec((1, tl, D_MODEL), lambda b, j: (b, j, 0)),
            pl.BlockSpec((1, D_MODEL), const),
            pl.BlockSpec((D_MODEL, W_SSD_COLS), const, pipeline_mode=pl.Buffered(1)),
            pl.BlockSpec((CONV_W, CONV_DIM), const),
            pl.BlockSpec((1, CONV_DIM), const),
            pl.BlockSpec((1, DT_PAD), const),
            pl.BlockSpec((1, DT_PAD), const),
            pl.BlockSpec((1, D_INNER), const),
            pl.BlockSpec((DT_PAD, D_INNER), const),
        ],
        out_specs=[
            pl.BlockSpec((1, tl, D_INNER), lambda b, j: (b, j, 0)),
            pl.BlockSpec((1, tl, D_INNER), lambda b, j: (b, j, 0)),
            pl.BlockSpec((1, D_INNER, D_STATE), lambda b, j: (b, 0, 0)),
            pl.BlockSpec((1, CONV_W - 1, CONV_DIM), lambda b, j: (b, 0, 0)),
        ],
        out_shape=[
            jax.ShapeDtypeStruct((n, s, D_INNER), BF16),
            jax.ShapeDtypeStruct((n, s, D_INNER), BF16),
            jax.ShapeDtypeStruct((n, D_INNER, D_STATE), F32),
            jax.ShapeDtypeStruct((n, CONV_W - 1, CONV_DIM), F32),
        ],
        scratch_shapes=[
            pltpu.VMEM((2, CHUNK + 8, CONV_DIM), F32),
            pltpu.VMEM((CHUNK, CONV_DIM), F32),
            pltpu.VMEM((2, CHUNK, DT_PAD), F32),
            pltpu.VMEM((D_STATE, D_INNER), F32),
            pltpu.VMEM((CHUNK, D_MODEL), BF16),
        ],
        compiler_params=pltpu.CompilerParams(
            dimension_semantics=("parallel", "arbitrary"), vmem_limit_bytes=VMEM_LIMIT),
        name="ssd_prompt",
    )(x, g1, w_ssd, conv_w, conv_b, dtb, alog, dskip_e, expand)


def _prep_ssd_params(norm1_g, w_in, conv_w, conv_b, dt_bias, a_log, d_skip):
    w_ssd = jnp.concatenate(
        [w_in[:, 0:OFF_DT], w_in[:, OFF_DT:OFF_Q], jnp.zeros((D_MODEL, DT_PAD - N_HEADS), w_in.dtype)],
        axis=1).astype(BF16)
    pad = lambda v: jnp.pad(v, (0, DT_PAD - N_HEADS)).reshape(1, DT_PAD)
    expand = (jnp.arange(DT_PAD)[:, None] == (jnp.arange(D_INNER) // HEAD_DIM)[None, :]).astype(BF16)
    return dict(
        g1=norm1_g.reshape(1, D_MODEL), w_ssd=w_ssd, conv_w=conv_w, conv_b=conv_b.reshape(1, CONV_DIM),
        dtb=pad(dt_bias), alog=pad(a_log), dskip_e=jnp.repeat(d_skip, HEAD_DIM).reshape(1, D_INNER),
        expand=expand)


QKV_COLS = 9 * ATTN_WIDTH
KV_WIDTH = 2 * ATTN_WIDTH
LANES = 128
ATTN_TILE = 512
ATTN_SUPER = WINDOWS[2]
SUB_STEPS = ATTN_SUPER // ATTN_TILE


def _alibi_slope(group, head):
    return 2.0 ** (-8.0 * (group * ATTN_HEADS + head + 1) / (3 * ATTN_HEADS))


def _banded_block(group, q, kv, ok):
    dil = DILATIONS[group]
    blk = ATTN_BLOCK
    delta = (lax.broadcasted_iota(jnp.int32, (blk, 2 * blk), 0) + blk
             - lax.broadcasted_iota(jnp.int32, (blk, 2 * blk), 1))
    dist = (dil * delta).astype(F32)
    lane_lo_q = lax.broadcasted_iota(jnp.int32, (blk, 2 * ATTN_DIM), 1) < ATTN_DIM
    lane_lo_v = lax.broadcasted_iota(jnp.int32, (2 * blk, 2 * ATTN_DIM), 1) < ATTN_DIM
    out = []
    for pr in range(ATTN_HEADS // 2):
        cols = slice(pr * 2 * ATTN_DIM, (pr + 1) * 2 * ATTN_DIM)
        vcols = slice(ATTN_WIDTH + pr * 2 * ATTN_DIM, ATTN_WIDTH + (pr + 1) * 2 * ATTN_DIM)
        q_pair, k_pair, v_pair = q[:, cols], kv[:, cols], kv[:, vcols]
        o_pair = None
        lse = []
        for k in range(2):
            sel_q = lane_lo_q if k == 0 else ~lane_lo_q
            sel_v = lane_lo_v if k == 0 else ~lane_lo_v
            q_h = jnp.where(sel_q, q_pair, jnp.zeros_like(q_pair))
            v_h = jnp.where(sel_v, v_pair, jnp.zeros_like(v_pair))
            s = _dot_nt(q_h, k_pair)
            s = jnp.where(ok, s - _alibi_slope(group, 2 * pr + k) * dist, NEG_BIG)
            m = jnp.max(s, axis=-1, keepdims=True)
            p = jnp.exp(s - m)
            den = jnp.sum(p, axis=-1, keepdims=True)
            o_h = _dot(p.astype(BF16), v_h) / den
            o_pair = o_h if o_pair is None else o_pair + o_h
            lse.append(m + jnp.log(den))
        out.append((o_pair, jnp.where(lane_lo_q, lse[0], lse[1])))
    return out


def _attn_prompt_kernel(x_ref, g1_ref, w_ref, o_ref, t0_ref, t1_ref, t2_ref,
                        nat, q0b, kv0b, q1b, kv1b, q2b, kv2b, o_nat, l_nat):
    j = pl.program_id(1)
    nj = pl.num_programs(1)
    p = j % SUB_STEPS
    blk = ATTN_BLOCK
    tm = ATTN_TILE
    qpan = ATTN_WIDTH // LANES
    kvpan = KV_WIDTH // LANES

    @pl.when(j == 0)
    def _():
        kv0b[0:blk, :] = jnp.zeros((blk, KV_WIDTH), BF16)
        kv1b[:, 0:blk, :] = jnp.zeros((DILATIONS[1], blk, KV_WIDTH), BF16)
        kv2b[:, 0:blk, :] = jnp.zeros((DILATIONS[2], blk, KV_WIDTH), BF16)

    @pl.when(j > 0)
    def _():
        kv0b[0:blk, :] = kv0b[tm:tm + blk, :]
        kv1b[:, 0:blk, :] = kv1b[:, blk:2 * blk, :]

    @pl.when((j > 0) & (p == 0))
    def _():
        kv2b[:, 0:blk, :] = kv2b[:, blk:2 * blk, :]

    h = _rmsnorm_rows(x_ref[0], g1_ref[...]).astype(BF16)
    for c in range(0, QKV_COLS // LANES, 2):
        res = _dot(h, w_ref[:, c * LANES:(c + 2) * LANES])
        if c < 3 * qpan:
            res = res * (ATTN_DIM ** -0.5)
        nat[c] = res[:, 0:LANES]
        nat[c + 1] = res[:, LANES:2 * LANES]

    for g, t_ref in enumerate((t0_ref, t1_ref, t2_ref)):
        keep = WINDOWS[g]
        kv_first = 3 * qpan + g * kvpan
        if keep >= tm:
            @pl.when(j >= nj - keep // tm)
            def _():
                for c in range(kvpan):
                    t_ref[0, :, c * LANES:(c + 1) * LANES] = nat[kv_first + c]
        else:
            @pl.when(j == nj - 1)
            def _():
                for c in range(kvpan):
                    t_ref[0, :, c * LANES:(c + 1) * LANES] = nat[kv_first + c, tm - keep:tm, :]

    for c in range(qpan):
        q0b[:, c * LANES:(c + 1) * LANES] = nat[c].astype(BF16)
    for c in range(kvpan):
        kv0b[blk:blk + tm, c * LANES:(c + 1) * LANES] = nat[3 * qpan + c].astype(BF16)
    d1 = DILATIONS[1]
    for r in range(d1):
        for c in range(qpan):
            q1b[r, :, c * LANES:(c + 1) * LANES] = nat[qpan + c, pl.ds(r, tm // d1, stride=d1), :].astype(BF16)
        for c in range(kvpan):
            kv1b[r, blk:2 * blk, c * LANES:(c + 1) * LANES] = (
                nat[3 * qpan + kvpan + c, pl.ds(r, tm // d1, stride=d1), :].astype(BF16))
    d2 = DILATIONS[2]
    sub = tm // d2
    row2 = pl.multiple_of(p * sub, sub)
    for r in range(d2):
        for c in range(qpan):
            q2b[r, pl.ds(row2, sub), c * LANES:(c + 1) * LANES] = (
                nat[2 * qpan + c, pl.ds(r, sub, stride=d2), :].astype(BF16))
        for c in range(kvpan):
            kv2b[r, pl.ds(blk + row2, sub), c * LANES:(c + 1) * LANES] = (
                nat[3 * qpan + 2 * kvpan + c, pl.ds(r, sub, stride=d2), :].astype(BF16))

    key_idx = lax.broadcasted_iota(jnp.int32, (blk, 2 * blk), 1)
    delta = lax.broadcasted_iota(jnp.int32, (blk, 2 * blk), 0) + blk - key_idx
    in_band = (delta >= 0) & (delta <= blk)
    own_key = key_idx >= blk
    base = pl.multiple_of(p * tm, tm)

    def store(group, rows, res):
        for pr, (o_pair, l_pair) in enumerate(res):
            o_nat[group, pr, rows, :] = o_pair
            l_nat[group, pr, rows, :] = l_pair

    for i in range(tm // blk):
        ok = in_band & (own_key | (j > 0)) if i == 0 else in_band
        res = _banded_block(0, q0b[i * blk:(i + 1) * blk, :], kv0b[i * blk:(i + 2) * blk, :], ok)
        store(0, pl.ds(base + i * blk, blk), res)

    ok1 = in_band & (own_key | (j > 0))
    for r in range(d1):
        store(1, pl.ds(base + r, blk, stride=d1), _banded_block(1, q1b[r], kv1b[r], ok1))

    @pl.when(p == SUB_STEPS - 1)
    def _():
        ok2 = in_band & (own_key | (j >= SUB_STEPS))
        unroll = 4

        def g2_body(it, carry):
            for k in range(unroll):
                r = it * unroll + k
                store(2, pl.ds(r, blk, stride=d2), _banded_block(2, q2b[r], kv2b[r], ok2))
            return carry

        lax.fori_loop(0, d2 // unroll, g2_body, 0)

        rb = 256
        for k in range(ATTN_SUPER // rb):
            rows = slice(k * rb, (k + 1) * rb)
            for pr in range(ATTN_HEADS // 2):
                l0, l1, l2 = l_nat[0, pr, rows, :], l_nat[1, pr, rows, :], l_nat[2, pr, rows, :]
                m = jnp.maximum(jnp.maximum(l0, l1), l2)
                e0, e1, e2 = jnp.exp(l0 - m), jnp.exp(l1 - m), jnp.exp(l2 - m)
                o = (e0 * o_nat[0, pr, rows, :] + e1 * o_nat[1, pr, rows, :]
                     + e2 * o_nat[2, pr, rows, :]) / (e0 + e1 + e2)
                o_ref[0, rows, pr * LANES:(pr + 1) * LANES] = o.astype(o_ref.dtype)


def _attn_prompt(x, g1, w_qkv):
    n, s, _ = x.shape
    tm = ATTN_TILE
    nj = s // tm
    assert s % ATTN_SUPER == 0
    const = lambda b, j: (0, 0)

    def tail_spec(keep):
        if keep >= tm:
            first = nj - keep // tm
            return pl.BlockSpec((1, tm, KV_WIDTH), lambda b, j: (b, jnp.maximum(j - first, 0), 0))
        return pl.BlockSpec((1, keep, KV_WIDTH), lambda b, j: (b, 0, 0))

    d1, d2 = DILATIONS[1], DILATIONS[2]
    pairs = ATTN_HEADS // 2
    return pl.pallas_call(
        _attn_prompt_kernel,
        grid=(n, nj),
        in_specs=[pl.BlockSpec((1, tm, D_MODEL), lambda b, j: (b, j, 0)), pl.BlockSpec((1, D_MODEL), const),
                  pl.BlockSpec((D_MODEL, QKV_COLS), const, pipeline_mode=pl.Buffered(1))],
        out_specs=[pl.BlockSpec((1, ATTN_SUPER, ATTN_WIDTH), lambda b, j: (b, j // SUB_STEPS, 0))]
        + [tail_spec(w) for w in WINDOWS],
        out_shape=[jax.ShapeDtypeStruct((n, s, ATTN_WIDTH), BF16)]
        + [jax.ShapeDtypeStruct((n, w, KV_WIDTH), F32) for w in WINDOWS],
        scratch_shapes=[
            pltpu.VMEM((QKV_COLS // LANES, tm, LANES), F32),
            pltpu.VMEM((tm, ATTN_WIDTH), BF16),
            pltpu.VMEM((ATTN_BLOCK + tm, KV_WIDTH), BF16),
            pltpu.VMEM((d1, tm // d1, ATTN_WIDTH), BF16),
            pltpu.VMEM((d1, ATTN_BLOCK + tm // d1, KV_WIDTH), BF16),
            pltpu.VMEM((d2, ATTN_SUPER // d2, ATTN_WIDTH), BF16),
            pltpu.VMEM((d2, ATTN_BLOCK + ATTN_SUPER // d2, KV_WIDTH), BF16),
            pltpu.VMEM((3, pairs, ATTN_SUPER, LANES), F32),
            pltpu.VMEM((3, pairs, ATTN_SUPER, LANES), F32),
        ],
        compiler_params=pltpu.CompilerParams(
            dimension_semantics=("parallel", "arbitrary"), vmem_limit_bytes=VMEM_LIMIT),
        name="attn_prompt",
    )(x, g1, w_qkv)


def _resident(shape):
    return pl.BlockSpec(shape, lambda *_: (0,) * len(shape), pipeline_mode=pl.Buffered(1))


def _merge_kernel(x_ref, yp_ref, z_ref, oa_ref, g1_ref, ng_ref, wg_ref, ws_ref, wa_ref, wo_ref, out_ref, ys_scr):
    xf = x_ref[...]
    h = _rmsnorm_rows(xf, g1_ref[...]).astype(BF16)
    gates = _dot(h, wg_ref[...])
    gates = 1.0 / (1.0 + jnp.exp(-gates))
    gw = D_INNER // N_GROUPS
    for g in range(N_GROUPS):
        cols = slice(g * gw, (g + 1) * gw)
        y = yp_ref[:, cols].astype(F32) * _silu(z_ref[:, cols].astype(F32))
        ys_scr[:, cols] = _rmsnorm_rows(y, ng_ref[:, cols]).astype(BF16)
    merged = (gates[:, 0:D_MODEL] * _dot(ys_scr[...], ws_ref[...])
              + gates[:, D_MODEL:2 * D_MODEL] * _dot(oa_ref[...].astype(BF16), wa_ref[...]))
    out_ref[...] = xf + _dot(merged.astype(BF16), wo_ref[...])


def _merge(x2d, y_pre, z, o_attn, g1, ng, w_gates, w_ssm_out, w_attn_out, w_o, *, tm):
    m = x2d.shape[0]
    row = lambda i: (i, 0)
    return pl.pallas_call(
        _merge_kernel,
        grid=(m // tm,),
        in_specs=[pl.BlockSpec((tm, D_MODEL), row), pl.BlockSpec((tm, D_INNER), row),
                  pl.BlockSpec((tm, D_INNER), row), pl.BlockSpec((tm, ATTN_WIDTH), row),
                  _resident((1, D_MODEL)), _resident((1, D_INNER)), _resident((D_MODEL, 2 * D_MODEL)),
                  _resident((D_INNER, D_MODEL)), _resident((ATTN_WIDTH, D_MODEL)),
                  _resident((D_MODEL, D_MODEL))],
        out_specs=pl.BlockSpec((tm, D_MODEL), row),
        out_shape=jax.ShapeDtypeStruct((m, D_MODEL), F32),
        scratch_shapes=[pltpu.VMEM((tm, D_INNER), BF16)],
        compiler_params=pltpu.CompilerParams(
            dimension_semantics=("parallel",), vmem_limit_bytes=VMEM_LIMIT),
        name="merge",
    )(x2d, y_pre, z, o_attn, g1, ng, w_gates, w_ssm_out, w_attn_out, w_o)


FF_CHUNK = 256


def _ffn_kernel(x_ref, g2_ref, gf_ref, wg_ref, wu_ref, wd_ref, out_ref, act_scr):
    xf = x_ref[...]
    h = _rmsnorm_rows(xf, g2_ref[...]).astype(BF16)
    for c in range(D_FF // FF_CHUNK):
        cols = slice(c * FF_CHUNK, (c + 1) * FF_CHUNK)
        gate = _dot(h, wg_ref[:, cols])
        act_scr[:, cols] = (_silu(gate) * _dot(h, wu_ref[:, cols])).astype(BF16)
    out_ref[...] = _rmsnorm_rows(xf + _dot(act_scr[...], wd_ref[...]), gf_ref[...])


def _ffn(x2d, g2, gf, w_gate, w_up, w_down, *, tm):
    m = x2d.shape[0]
    row = lambda i: (i, 0)
    return pl.pallas_call(
        _ffn_kernel,
        grid=(m // tm,),
        in_specs=[pl.BlockSpec((tm, D_MODEL), row), _resident((1, D_MODEL)), _resident((1, D_MODEL)),
                  _resident((D_MODEL, D_FF)), _resident((D_MODEL, D_FF)), _resident((D_FF, D_MODEL))],
        out_specs=pl.BlockSpec((tm, D_MODEL), row),
        out_shape=jax.ShapeDtypeStruct((m, D_MODEL), F32),
        scratch_shapes=[pltpu.VMEM((tm, D_FF), BF16)],
        compiler_params=pltpu.CompilerParams(
            dimension_semantics=("parallel",), vmem_limit_bytes=VMEM_LIMIT),
        name="ffn",
    )(x2d, g2, gf, w_gate, w_up, w_down)


SAMPLE_BLOCK = 8


def _ssd_sample_pre_kernel(x_ref, g1_ref, w_ref, cs_ref, cw_ref, cb_ref, dtb_ref, alog_ref, e_ref,
                           z_ref, xs_ref, b_ref, c_ref, xdt_t_ref, da_t_ref, conv_out_ref):
    h = _rmsnorm_rows(x_ref[...], g1_ref[...]).astype(BF16)
    z_ref[...] = _dot(h, w_ref[:, 0:D_INNER])
    xbc = _dot(h, w_ref[:, D_INNER:D_INNER + CONV_DIM])
    dt_raw = _dot(h, w_ref[:, D_INNER + CONV_DIM:W_SSD_COLS])
    acc = cb_ref[...] + cw_ref[CONV_W - 1:CONV_W, :] * xbc
    for k in range(CONV_W - 1):
        hist = cs_ref[k]
        acc = acc + cw_ref[k:k + 1, :] * hist
        if k > 0:
            conv_out_ref[k - 1] = hist
    conv_out_ref[CONV_W - 2] = xbc
    xc = _silu(acc)
    xs = xc[:, 0:D_INNER]
    xs_ref[...] = xs
    b_ref[...] = xc[:, D_INNER:D_INNER + N_GROUPS * D_STATE]
    c_ref[...] = xc[:, D_INNER + N_GROUPS * D_STATE:CONV_DIM]
    dt = _softplus(dt_raw + dtb_ref[...])
    da = jnp.exp(dt * (-jnp.exp(alog_ref[...])))
    xdt = xs * _dot_f32_lhs(dt, e_ref[...])
    da_e = _dot_f32_lhs(da, e_ref[...])
    nb = x_ref.shape[0]
    for k in range(D_INNER // nb):
        xdt_t_ref[k * nb:(k + 1) * nb, :] = xdt[:, k * nb:(k + 1) * nb].T
        da_t_ref[k * nb:(k + 1) * nb, :] = da_e[:, k * nb:(k + 1) * nb].T


def _ssd_sample_pre(x2d, conv_state_t, p):
    nb = x2d.shape[0]
    full = lambda shape: pl.BlockSpec(shape, lambda i: (0,) * len(shape))
    return pl.pallas_call(
        _ssd_sample_pre_kernel,
        grid=(1,),
        in_specs=[full((nb, D_MODEL)), full((1, D_MODEL)), full((D_MODEL, W_SSD_COLS)),
                  full((CONV_W - 1, nb, CONV_DIM)), full((CONV_W, CONV_DIM)), full((1, CONV_DIM)),
                  full((1, DT_PAD)), full((1, DT_PAD)), full((DT_PAD, D_INNER))],
        out_specs=[full((nb, D_INNER)), full((nb, D_INNER)), full((nb, N_GROUPS * D_STATE)),
                   full((nb, N_GROUPS * D_STATE)), full((D_INNER, nb)), full((D_INNER, nb)),
                   full((CONV_W - 1, nb, CONV_DIM))],
        out_shape=[jax.ShapeDtypeStruct((nb, D_INNER), F32), jax.ShapeDtypeStruct((nb, D_INNER), F32),
                   jax.ShapeDtypeStruct((nb, N_GROUPS * D_STATE), F32),
                   jax.ShapeDtypeStruct((nb, N_GROUPS * D_STATE), F32),
                   jax.ShapeDtypeStruct((D_INNER, nb), F32), jax.ShapeDtypeStruct((D_INNER, nb), F32),
                   jax.ShapeDtypeStruct((CONV_W - 1, nb, CONV_DIM), F32)],
        compiler_params=pltpu.CompilerParams(vmem_limit_bytes=VMEM_LIMIT),
        name="ssd_sample_pre",
    )(x2d, p["g1"], p["w_ssd"], conv_state_t, p["conv_w"], p["conv_b"], p["dtb"], p["alog"], p["expand"])


def _ssd_sample_state_kernel(st_ref, xdt_ref, da_ref, b_ref, c_ref, xs_ref, dskip_ref,
                             st_out_ref, y_ref, y_scr):
    grows = D_INNER // N_GROUPS
    for i in range(SAMPLE_BLOCK):
        for g in range(N_GROUPS):
            rows = slice(g * grows, (g + 1) * grows)
            s_new = (st_ref[i, rows, :] * da_ref[0, rows, i:i + 1]
                     + xdt_ref[0, rows, i:i + 1] * b_ref[i, g:g + 1, :])
            st_out_ref[i, rows, :] = s_new
            y_t = _dot_nt(c_ref[i].astype(BF16), s_new.astype(BF16))
            y_scr[i:i + 1, rows] = y_t[g:g + 1, :]
    y_ref[...] = (y_scr[...] + dskip_ref[...] * xs_ref[...]).astype(y_ref.dtype)


def _ssd_sample_state(state, xdt3, da3, b3, c3, xs, p):
    nb = state.shape[0]
    bb = SAMPLE_BLOCK
    const = lambda i: (0, 0)
    return pl.pallas_call(
        _ssd_sample_state_kernel,
        grid=(nb // bb,),
        in_specs=[pl.BlockSpec((bb, D_INNER, D_STATE), lambda i: (i, 0, 0)),
                  pl.BlockSpec((1, D_INNER, bb), lambda i: (i, 0, 0)),
                  pl.BlockSpec((1, D_INNER, bb), lambda i: (i, 0, 0)),
                  pl.BlockSpec((bb, N_GROUPS, D_STATE), lambda i: (i, 0, 0)),
                  pl.BlockSpec((bb, N_GROUPS, D_STATE), lambda i: (i, 0, 0)),
                  pl.BlockSpec((bb, D_INNER), lambda i: (i, 0)),
                  pl.BlockSpec((1, D_INNER), const)],
        out_specs=[pl.BlockSpec((bb, D_INNER, D_STATE), lambda i: (i, 0, 0)),
                   pl.BlockSpec((bb, D_INNER), lambda i: (i, 0))],
        out_shape=[jax.ShapeDtypeStruct((nb, D_INNER, D_STATE), F32),
                   jax.ShapeDtypeStruct((nb, D_INNER), BF16)],
        scratch_shapes=[pltpu.VMEM((bb, D_INNER), F32)],
        compiler_params=pltpu.CompilerParams(
            dimension_semantics=("parallel",), vmem_limit_bytes=VMEM_LIMIT),
        name="ssd_sample_state",
    )(state, xdt3, da3, b3, c3, xs, p["dskip_e"])


ATTN_SAMPLE_BLOCK = 2


def _proj_rows_kernel(x_ref, g1_ref, w_ref, wt_ref, out_ref, out_t_ref):
    h = _rmsnorm_rows(x_ref[...], g1_ref[...]).astype(BF16)
    out_ref[...] = _dot(h, w_ref[...])
    out_t_ref[...] = _dot_nt(wt_ref[...], h)


def _proj_rows(x2d, g1, w, w_t):
    nb, cols = x2d.shape[0], w.shape[1]
    full = lambda shape: pl.BlockSpec(shape, lambda i: (0,) * len(shape))
    return pl.pallas_call(
        _proj_rows_kernel,
        grid=(1,),
        in_specs=[full((nb, D_MODEL)), full((1, D_MODEL)), full((D_MODEL, cols)), full((cols, D_MODEL))],
        out_specs=[full((nb, cols)), full((cols, nb))],
        out_shape=[jax.ShapeDtypeStruct((nb, cols), F32), jax.ShapeDtypeStruct((cols, nb), F32)],
        compiler_params=pltpu.CompilerParams(vmem_limit_bytes=VMEM_LIMIT),
        name="proj_rows",
    )(x2d, g1, w, w_t)


def _attn_sample_kernel(qkv_t_ref, c0_ref, c1_ref, c2_ref, o_t_ref):
    kv0 = 3 * ATTN_WIDTH
    for i in range(ATTN_SAMPLE_BLOCK):
        lane = slice(i, i + 1)
        for h in range(ATTN_HEADS):
            hrows = slice(h * ATTN_DIM, (h + 1) * ATTN_DIM)
            scores, new_scores = [], []
            for g, c_ref in enumerate((c0_ref, c1_ref, c2_ref)):
                w, dil = WINDOWS[g], DILATIONS[g]
                q_col = qkv_t_ref[0, g * ATTN_WIDTH + h * ATTN_DIM:g * ATTN_WIDTH + (h + 1) * ATTN_DIM, lane]
                q_col = q_col * (ATTN_DIM ** -0.5)
                k_lo = kv0 + g * KV_WIDTH + h * ATTN_DIM
                k_col = qkv_t_ref[0, k_lo:k_lo + ATTN_DIM, lane]
                s = jnp.sum(c_ref[i, hrows, :] * q_col, axis=0, keepdims=True)
                pos = lax.broadcasted_iota(jnp.int32, (1, w), 1)
                on_stride = (pos & (dil - 1)) == 0
                s = jnp.where(on_stride, s - _alibi_slope(g, h) * (w - pos).astype(F32), NEG_BIG)
                scores.append(s)
                new_scores.append(jnp.sum(q_col * k_col, axis=0, keepdims=True))
            m = new_scores[0]
            for g in range(3):
                m = jnp.maximum(m, jnp.maximum(new_scores[g], jnp.max(scores[g], axis=-1, keepdims=True)))
            den = jnp.zeros((1, 1), F32)
            acc = jnp.zeros((ATTN_DIM, 1), F32)
            for g, c_ref in enumerate((c0_ref, c1_ref, c2_ref)):
                p = jnp.exp(scores[g] - m)
                p_new = jnp.exp(new_scores[g] - m)
                den = den + jnp.sum(p, axis=-1, keepdims=True) + p_new
                v_lo = kv0 + g * KV_WIDTH + ATTN_WIDTH + h * ATTN_DIM
                v_col = qkv_t_ref[0, v_lo:v_lo + ATTN_DIM, lane]
                v_t = c_ref[i, ATTN_WIDTH + h * ATTN_DIM:ATTN_WIDTH + (h + 1) * ATTN_DIM, :]
                acc = acc + jnp.sum(v_t * p, axis=-1, keepdims=True) + p_new * v_col
            o_t_ref[0, hrows, lane] = acc / den


def _attn_sample(qkv_t, caches):
    nb = qkv_t.shape[1]
    bb = ATTN_SAMPLE_BLOCK
    views = []
    for g, c in enumerate(caches):
        assert c.shape[1] == WINDOWS[g]
        views.append(jnp.transpose(c, (0, 2, 3, 4, 1)).reshape(nb, KV_WIDTH, WINDOWS[g]))
    qkv_blocks = qkv_t.reshape(QKV_COLS, nb // bb, bb).transpose(1, 0, 2)
    o_t = pl.pallas_call(
        _attn_sample_kernel,
        grid=(nb // bb,),
        in_specs=[pl.BlockSpec((1, QKV_COLS, bb), lambda i: (i, 0, 0))]
        + [pl.BlockSpec((bb, KV_WIDTH, w), lambda i: (i, 0, 0)) for w in WINDOWS],
        out_specs=pl.BlockSpec((1, ATTN_WIDTH, bb), lambda i: (i, 0, 0)),
        out_shape=jax.ShapeDtypeStruct((nb // bb, ATTN_WIDTH, bb), F32),
        compiler_params=pltpu.CompilerParams(
            dimension_semantics=("parallel",), vmem_limit_bytes=VMEM_LIMIT),
        name="attn_sample",
    )(qkv_blocks, *views)
    return o_t.transpose(0, 2, 1).reshape(nb, ATTN_WIDTH)


def _layer_sample(x2d, state_ssm, state_conv, caches, wp):
    nb = x2d.shape[0]
    p = wp["ssd"]
    bb = SAMPLE_BLOCK
    z, xs, bm, cm, xdt_t, da_t, conv_new = _ssd_sample_pre(
        x2d, jnp.transpose(state_conv, (1, 0, 2)), p)
    to_blocks = lambda t: t.reshape(D_INNER, nb // bb, bb).transpose(1, 0, 2)
    st_new, y_pre = _ssd_sample_state(
        state_ssm.reshape(nb, D_INNER, D_STATE), to_blocks(xdt_t), to_blocks(da_t),
        bm.reshape(nb, N_GROUPS, D_STATE), cm.reshape(nb, N_GROUPS, D_STATE), xs, p)
    qkv, qkv_t = _proj_rows(x2d, p["g1"], wp["w_qkv"], wp["w_qkv"].T)
    kv_new = qkv[:, 3 * ATTN_WIDTH:]
    o = _attn_sample(qkv_t, caches)
    x1 = _merge(x2d, y_pre, z, o, p["g1"], wp["ng"], wp["w_gates"], wp["w_ssm_out"], wp["w_attn_out"],
                wp["w_o"], tm=nb)
    y = _ffn(x1, wp["g2"], wp["gf"], wp["w_ffn_gate"], wp["w_ffn_up"], wp["w_ffn_down"], tm=nb)
    return y, st_new, conv_new, kv_new


def _prep_attn_weight(w_in):
    cols = [w_in[:, OFF_Q:OFF_Q + 3 * ATTN_WIDTH]]
    for g in range(3):
        cols.append(w_in[:, OFF_K + g * ATTN_WIDTH:OFF_K + (g + 1) * ATTN_WIDTH])
        cols.append(w_in[:, OFF_V + g * ATTN_WIDTH:OFF_V + (g + 1) * ATTN_WIDTH])
    return jnp.concatenate(cols, axis=1).astype(BF16)


def _layer_prompt(x, wp):
    n, s, _ = x.shape
    y_pre, z, st, conv = _ssd_prompt(x, **wp["ssd"])
    o_attn, t0, t1, t2 = _attn_prompt(x, wp["ssd"]["g1"], wp["w_qkv"])
    x1 = _merge(x.reshape(n * s, D_MODEL), y_pre.reshape(n * s, D_INNER), z.reshape(n * s, D_INNER),
                o_attn.reshape(n * s, ATTN_WIDTH), wp["ssd"]["g1"], wp["ng"],
                wp["w_gates"], wp["w_ssm_out"], wp["w_attn_out"], wp["w_o"], tm=256)
    y = _ffn(x1, wp["g2"], wp["gf"], wp["w_ffn_gate"], wp["w_ffn_up"], wp["w_ffn_down"], tm=512)
    return y.reshape(n, s, D_MODEL), st, conv, (t0, t1, t2)


def _prep_weights(norm1_g, w_in, conv_w, conv_b, dt_bias, a_log, d_skip, ssm_norm_g, w_ssm_out, w_attn_out,
                  w_o, norm2_g, w_ffn_gate, w_ffn_up, w_ffn_down, norm_f_g):
    return dict(
        ssd=_prep_ssd_params(norm1_g, w_in, conv_w, conv_b, dt_bias, a_log, d_skip),
        w_qkv=_prep_attn_weight(w_in), ng=ssm_norm_g.reshape(1, D_INNER),
        w_gates=w_in[:, OFF_GS:OFF_GS + 2 * D_MODEL].astype(BF16),
        w_ssm_out=w_ssm_out.astype(BF16), w_attn_out=w_attn_out.astype(BF16), w_o=w_o.astype(BF16),
        g2=norm2_g.reshape(1, D_MODEL), gf=norm_f_g.reshape(1, D_MODEL),
        w_ffn_gate=w_ffn_gate.astype(BF16), w_ffn_up=w_ffn_up.astype(BF16),
        w_ffn_down=w_ffn_down.astype(BF16))


def kernel(x_prompt, x_sample, state_ssm, state_conv, cache_kv_w128, cache_kv_w512, cache_kv_w2048, norm1_g, w_in, conv_w, conv_b, dt_bias, a_log, d_skip, ssm_norm_g, w_ssm_out, w_attn_out, w_o, norm2_g, w_ffn_gate, w_ffn_up, w_ffn_down, norm_f_g):
    wp = _prep_weights(norm1_g[0], w_in[0], conv_w[0], conv_b[0], dt_bias[0], a_log[0], d_skip[0],
                       ssm_norm_g[0], w_ssm_out[0], w_attn_out[0], w_o[0], norm2_g[0], w_ffn_gate[0],
                       w_ffn_up[0], w_ffn_down[0], norm_f_g)
    n, s, _ = x_prompt.shape
    nb = x_sample.shape[0]
    y_p, st_p, conv_p, tails = _layer_prompt(x_prompt, wp)
    y_s, st_s, conv_s, kv_s = _layer_sample(
        x_sample.reshape(nb, D_MODEL), state_ssm[0], state_conv[0],
        (cache_kv_w128[0], cache_kv_w512[0], cache_kv_w2048[0]), wp)
    kv_tail_shape = (2, ATTN_HEADS, ATTN_DIM)
    outs = [y_p, y_s.reshape(nb, 1, D_MODEL),
            st_p.reshape(1, n, N_HEADS, HEAD_DIM, D_STATE), st_s.reshape(1, nb, N_HEADS, HEAD_DIM, D_STATE),
            conv_p.reshape(1, n, CONV_W - 1, CONV_DIM), jnp.transpose(conv_s, (1, 0, 2)).reshape(1, nb, CONV_W - 1, CONV_DIM)]
    for g in range(3):
        outs.append(tails[g].reshape((1, n, WINDOWS[g]) + kv_tail_shape))
        outs.append(kv_s[:, g * KV_WIDTH:(g + 1) * KV_WIDTH].reshape((1, nb, 1) + kv_tail_shape))
    return tuple(outs)
```

```python
import functools

import jax
import jax.numpy as jnp
from jax import lax
from jax.experimental import pallas as pl
from jax.experimental.pallas import tpu as pltpu

F32 = jnp.float32
BF16 = jnp.bfloat16

D_MODEL = 1024
D_INNER = 2048
HEAD_DIM = 64
N_HEADS = 32
N_GROUPS = 4
HEADS_PER_SSM_GROUP = N_HEADS // N_GROUPS
D_STATE = 128
CONV_W = 4
CONV_DIM = D_INNER + 2 * N_GROUPS * D_STATE
CHUNK = 128
DT_PAD = 128
ATTN_HEADS = 4
ATTN_DIM = 64
ATTN_WIDTH = ATTN_HEADS * ATTN_DIM
WINDOWS = (128, 512, 2048)
DILATIONS = (1, 4, 16)
ATTN_BLOCK = 128
D_FF = 2816
RMS_EPS = 1e-6
NEG_BIG = -1e30
OFF_XBC = D_INNER
OFF_DT = OFF_XBC + CONV_DIM
OFF_Q = OFF_DT + N_HEADS
OFF_K = OFF_Q + 3 * ATTN_WIDTH
OFF_V = OFF_K + 3 * ATTN_WIDTH
OFF_GS = OFF_V + 3 * ATTN_WIDTH
W_SSD_COLS = D_INNER + CONV_DIM + DT_PAD
VMEM_LIMIT = 52 * 1024 * 1024


def _dot(a, b):
    return jnp.dot(a, b, preferred_element_type=F32)


def _dot_nt(a, b):
    return lax.dot_general(a, b, (((1,), (1,)), ((), ())), preferred_element_type=F32)


def _split3(v):
    hi = v.astype(BF16)
    r1 = v - hi.astype(F32)
    mid = r1.astype(BF16)
    lo = (r1 - mid.astype(F32)).astype(BF16)
    return hi, mid, lo


def _dot_f32_rhs(a_bf16_exact, v):
    hi, mid, lo = _split3(v)
    return _dot(a_bf16_exact, hi) + _dot(a_bf16_exact, mid) + _dot(a_bf16_exact, lo)


def _dot_f32_lhs(v, b_bf16_exact):
    hi, mid, lo = _split3(v)
    return _dot(hi, b_bf16_exact) + _dot(mid, b_bf16_exact) + _dot(lo, b_bf16_exact)


def _rmsnorm_rows(xf, g_row):
    ms = jnp.mean(xf * xf, axis=-1, keepdims=True)
    return xf * lax.rsqrt(ms + RMS_EPS) * g_row


def _silu(v):
    h = 0.5 * v
    return h + h * jnp.tanh(h)


def _log1p(e):
    u = 1.0 + e
    return jnp.where(u == 1.0, e, jnp.log(u) * (e / (u - 1.0)))


def _softplus(v):
    return jnp.maximum(v, 0.0) + _log1p(jnp.exp(-jnp.abs(v)))


def _ssd_chunk(out_rows, xc_ref, xslot, dt_ref, dslot, st_ref, y_ref, e_ref, dtb_row, a_row, dskip_row,
               fillers):
    tri_r = lax.broadcasted_iota(jnp.int32, (CHUNK, CHUNK), 0)
    tri_c = lax.broadcasted_iota(jnp.int32, (CHUNK, CHUNK), 1)
    causal = tri_r >= tri_c
    ltri = causal.astype(BF16)
    lane_lo = tri_c < HEAD_DIM
    fillers = list(fillers)
    n_slots = N_HEADS // 2

    def run_fillers(slots_left):
        for _ in range(-(-len(fillers) // slots_left)):
            fillers.pop(0)()

    dt = _softplus(dt_ref[dslot] + dtb_row)
    adt = dt * a_row
    acum = _dot_f32_rhs(ltri, adt)
    acum_dt_t = (acum - jnp.log(dt)).T
    a_last = acum[CHUNK - 1:CHUNK, :]
    w_t = (dt * jnp.exp(a_last - acum)).T
    cdec = jnp.exp(a_last)
    cdec_e = _dot_f32_lhs(jnp.broadcast_to(cdec, (8, DT_PAD)), e_ref[...])[0:1, :]

    for g in range(N_GROUPS):
        b_g = xc_ref[xslot, :, D_INNER + g * D_STATE:D_INNER + (g + 1) * D_STATE]
        c_g = xc_ref[xslot, :, D_INNER + N_GROUPS * D_STATE + g * D_STATE:
                     D_INNER + N_GROUPS * D_STATE + (g + 1) * D_STATE]
        b_bf = b_g.astype(BF16)
        c_bf = c_g.astype(BF16)
        cb = _dot_nt(c_bf, b_bf)
        b_t = b_g.T
        gcols = slice(g * HEADS_PER_SSM_GROUP * HEAD_DIM, (g + 1) * HEADS_PER_SSM_GROUP * HEAD_DIM)
        y_off = _dot(c_bf, st_ref[:, gcols].astype(BF16))
        for pr in range(HEADS_PER_SSM_GROUP // 2):
            h0 = g * HEADS_PER_SSM_GROUP + 2 * pr
            pcols = slice(h0 * HEAD_DIM, (h0 + 2) * HEAD_DIM)
            xs_f32 = xc_ref[xslot, :, pcols]
            xs_pair = xs_f32.astype(BF16)
            xs_diag = jnp.concatenate([jnp.where(lane_lo, xs_pair, jnp.zeros_like(xs_pair)),
                                       jnp.where(lane_lo, jnp.zeros_like(xs_pair), xs_pair)], axis=0)
            a_b, wmats, smats = [], [], []
            for k in range(2):
                h = h0 + k
                ab = jnp.broadcast_to(acum[:, h:h + 1], (CHUNK, CHUNK))
                a_b.append(ab)
                dec = jnp.exp(jnp.where(causal, ab - acum_dt_t[h:h + 1, :], NEG_BIG))
                wmats.append((cb * dec).astype(BF16))
                smats.append((b_t * w_t[h:h + 1, :]).astype(BF16))
            y_pair = _dot(jnp.concatenate(wmats, axis=1), xs_diag)
            s_pair = _dot(jnp.concatenate(smats, axis=1), xs_diag)
            ea_pair = jnp.exp(jnp.where(lane_lo, a_b[0], a_b[1]))
            lo = (2 * pr) * HEAD_DIM
            y_pre = y_pair + y_off[:, lo:lo + 2 * HEAD_DIM] * ea_pair + dskip_row[:, pcols] * xs_f32
            y_ref[0, out_rows, pcols] = y_pre.astype(y_ref.dtype)
            st_ref[:, pcols] = st_ref[:, pcols] * cdec_e[:, pcols] + s_pair
            run_fillers(n_slots - (g * (HEADS_PER_SSM_GROUP // 2) + pr))
    assert not fillers


def _ssd_prompt_kernel(x_ref, g1_ref, w_ref, cw_ref, cb_ref, dtb_ref, alog_ref, dskip_ref,
                       e_ref, y_ref, z_out_ref, st_out_ref, conv_out_ref,
                       ext_ref, xc_ref, dt_ref, st_ref, h_scr, *, tl):
    j = pl.program_id(1)
    nj = pl.num_programs(1)
    nchunk = tl // CHUNK
    hist = 8

    @pl.when(j == 0)
    def _():
        st_ref[...] = jnp.zeros_like(st_ref)
        ext_ref[1, CHUNK:CHUNK + hist, :] = jnp.zeros((hist, CONV_DIM), F32)

    dtb_row = dtb_ref[...]
    a_row = -jnp.exp(alog_ref[...])
    dskip_row = dskip_ref[...]
    piece = 512
    pieces = ([("z", lo, lo + piece) for lo in range(0, D_INNER, piece)]
              + [("xbc", lo, lo + piece) for lo in range(0, CONV_DIM, piece)] + [("dt", 0, DT_PAD)])
    cblk = 256

    def rows_of(k):
        return pl.ds(pl.multiple_of(k * CHUNK, CHUNK), CHUNK)

    def project_thunks(k, slot):
        rows = rows_of(k)

        def normalize():
            h_scr[...] = _rmsnorm_rows(x_ref[0, rows, :], g1_ref[...]).astype(BF16)
            ext_ref[slot, 0:hist, :] = ext_ref[1 - slot, CHUNK:CHUNK + hist, :]

        def piece_thunk(kind, lo, hi):
            def run():
                if kind == "z":
                    z_out_ref[0, rows, lo:hi] = _dot(h_scr[...], w_ref[:, lo:hi]).astype(z_out_ref.dtype)
                elif kind == "xbc":
                    ext_ref[slot, hist:hist + CHUNK, lo:hi] = _dot(
                        h_scr[...], w_ref[:, D_INNER + lo:D_INNER + hi])
                else:
                    dt_ref[slot] = _dot(h_scr[...], w_ref[:, D_INNER + CONV_DIM:W_SSD_COLS])
            return run

        return [normalize] + [piece_thunk(*p) for p in pieces]

    def conv_thunks(slot):
        def block_thunk(cbk):
            def run():
                cols = slice(cbk * cblk, (cbk + 1) * cblk)
                win = ext_ref[slot, :, cols]
                acc = cb_ref[:, cols] + cw_ref[CONV_W - 1:CONV_W, cols] * win[hist:, :]
                for t in range(1, CONV_W):
                    tap = cw_ref[CONV_W - 1 - t:CONV_W - t, cols]
                    acc = acc + tap * pltpu.roll(win, t, axis=0)[hist:, :]
                xc_ref[slot, :, cols] = _silu(acc)
            return run

        return [block_thunk(c) for c in range(CONV_DIM // cblk)]

    def interleave(a, b):
        out = []
        for i in range(max(len(a), len(b))):
            out += a[i:i + 1] + b[i:i + 1]
        return out

    def scan(k, slot, fillers):
        _ssd_chunk(rows_of(k), xc_ref, slot, dt_ref, slot, st_ref, y_ref, e_ref,
                   dtb_row, a_row, dskip_row, fillers)

    for thunk in project_thunks(0, 0) + interleave(conv_thunks(0), project_thunks(1, 1)):
        thunk()

    def body(m, carry):
        for e in range(2):
            k = 2 * m + e
            scan(k, e, interleave(conv_thunks(1 - e), project_thunks(k + 2, e)))
        return carry

    lax.fori_loop(0, (nchunk - 2) // 2, body, 0)
    scan(nchunk - 2, 0, conv_thunks(1))
    scan(nchunk - 1, 1, [])

    @pl.when(j == nj - 1)
    def _():
        last = (nchunk - 1) % 2
        conv_out_ref[0] = ext_ref[last, hist + CHUNK - (CONV_W - 1):hist + CHUNK, :]
        for k in range(D_INNER // D_STATE):
            st_out_ref[0, k * D_STATE:(k + 1) * D_STATE, :] = st_ref[:, k * D_STATE:(k + 1) * D_STATE].T


def _ssd_prompt(x, g1, w_ssd, conv_w, conv_b, dtb, alog, dskip_e, expand, *, tl=1024):
    n, s, _ = x.shape
    assert (tl // CHUNK) % 2 == 0 and s % tl == 0
    kern = functools.partial(_ssd_prompt_kernel, tl=tl)
    const = lambda b, j: (0, 0)
    return pl.pallas_call(
        kern,
        grid=(n, s // tl),
        in_specs=[
            pl.BlockSpec((1, tl, D_MODEL), lambda b, j: (b, j, 0)),
            pl.BlockSpec((1, D_MODEL), const),
            pl.BlockSpec((D_MODEL, W_SSD_COLS), const, pipeline_mode=pl.Buffered(1)),
            pl.BlockSpec((CONV_W, CONV_DIM), const),
            pl.BlockSpec((1, CONV_DIM), const),
            pl.BlockSpec((1, DT_PAD), const),
            pl.BlockSpec((1, DT_PAD), const),
            pl.BlockSpec((1, D_INNER), const),
            pl.BlockSpec((DT_PAD, D_INNER), const),
        ],
        out_specs=[
            pl.BlockSpec((1, tl, D_INNER), lambda b, j: (b, j, 0)),
            pl.BlockSpec((1, tl, D_INNER), lambda b, j: (b, j, 0)),
            pl.BlockSpec((1, D_INNER, D_STATE), lambda b, j: (b, 0, 0)),
            pl.BlockSpec((1, CONV_W - 1, CONV_DIM), lambda b, j: (b, 0, 0)),
        ],
        out_shape=[
            jax.ShapeDtypeStruct((n, s, D_INNER), BF16),
            jax.ShapeDtypeStruct((n, s, D_INNER), BF16),
            jax.ShapeDtypeStruct((n, D_INNER, D_STATE), F32),
            jax.ShapeDtypeStruct((n, CONV_W - 1, CONV_DIM), F32),
        ],
        scratch_shapes=[
            pltpu.VMEM((2, CHUNK + 8, CONV_DIM), F32),
            pltpu.VMEM((2, CHUNK, CONV_DIM), F32),
            pltpu.VMEM((2, CHUNK, DT_PAD), F32),
            pltpu.VMEM((D_STATE, D_INNER), F32),
            pltpu.VMEM((CHUNK, D_MODEL), BF16),
        ],
        compiler_params=pltpu.CompilerParams(
            dimension_semantics=("parallel", "arbitrary"), vmem_limit_bytes=VMEM_LIMIT),
        name="ssd_prompt",
    )(x, g1, w_ssd, conv_w, conv_b, dtb, alog, dskip_e, expand)


def _prep_ssd_params(norm1_g, w_in, conv_w, conv_b, dt_bias, a_log, d_skip):
    w_ssd = jnp.concatenate(
        [w_in[:, 0:OFF_DT], w_in[:, OFF_DT:OFF_Q], jnp.zeros((D_MODEL, DT_PAD - N_HEADS), w_in.dtype)],
        axis=1).astype(BF16)
    pad = lambda v: jnp.pad(v, (0, DT_PAD - N_HEADS)).reshape(1, DT_PAD)
    expand = (jnp.arange(DT_PAD)[:, None] == (jnp.arange(D_INNER) // HEAD_DIM)[None, :]).astype(BF16)
    return dict(
        g1=norm1_g.reshape(1, D_MODEL), w_ssd=w_ssd, conv_w=conv_w, conv_b=conv_b.reshape(1, CONV_DIM),
        dtb=pad(dt_bias), alog=pad(a_log), dskip_e=jnp.repeat(d_skip, HEAD_DIM).reshape(1, D_INNER),
        expand=expand)


QKV_COLS = 9 * ATTN_WIDTH
KV_WIDTH = 2 * ATTN_WIDTH
LANES = 128
ATTN_TILE = 512
ATTN_SUPER = WINDOWS[2]
SUB_STEPS = ATTN_SUPER // ATTN_TILE


def _alibi_slope(group, head):
    return 2.0 ** (-8.0 * (group * ATTN_HEADS + head + 1) / (3 * ATTN_HEADS))


def _banded_block(group, q, kv, ok):
    dil = DILATIONS[group]
    blk = ATTN_BLOCK
    delta = (lax.broadcasted_iota(jnp.int32, (blk, 2 * blk), 0) + blk
             - lax.broadcasted_iota(jnp.int32, (blk, 2 * blk), 1))
    dist = (dil * delta).astype(F32)
    lane_lo_q = lax.broadcasted_iota(jnp.int32, (blk, 2 * ATTN_DIM), 1) < ATTN_DIM
    lane_lo_v = lax.broadcasted_iota(jnp.int32, (2 * blk, 2 * ATTN_DIM), 1) < ATTN_DIM
    out = []
    for pr in range(ATTN_HEADS // 2):
        cols = slice(pr * 2 * ATTN_DIM, (pr + 1) * 2 * ATTN_DIM)
        vcols = slice(ATTN_WIDTH + pr * 2 * ATTN_DIM, ATTN_WIDTH + (pr + 1) * 2 * ATTN_DIM)
        q_pair, k_pair, v_pair = q[:, cols], kv[:, cols], kv[:, vcols]
        o_pair = None
        lse = []
        for k in range(2):
            sel_q = lane_lo_q if k == 0 else ~lane_lo_q
            sel_v = lane_lo_v if k == 0 else ~lane_lo_v
            q_h = jnp.where(sel_q, q_pair, jnp.zeros_like(q_pair))
            v_h = jnp.where(sel_v, v_pair, jnp.zeros_like(v_pair))
            s = _dot_nt(q_h, k_pair)
            s = jnp.where(ok, s - _alibi_slope(group, 2 * pr + k) * dist, NEG_BIG)
            m = jnp.max(s, axis=-1, keepdims=True)
            p = jnp.exp(s - m)
            den = jnp.sum(p, axis=-1, keepdims=True)
            o_h = _dot(p.astype(BF16), v_h) / den
            o_pair = o_h if o_pair is None else o_pair + o_h
            lse.append(m + jnp.log(den))
        out.append((o_pair, jnp.where(lane_lo_q, lse[0], lse[1])))
    return out


def _attn_prompt_kernel(x_ref, g1_ref, w_ref, o_ref, t0_ref, t1_ref, t2_ref,
                        nat, q0b, kv0b, q1b, kv1b, q2b, kv2b, o_nat, l_nat):
    j = pl.program_id(1)
    nj = pl.num_programs(1)
    p = j % SUB_STEPS
    blk = ATTN_BLOCK
    tm = ATTN_TILE
    qpan = ATTN_WIDTH // LANES
    kvpan = KV_WIDTH // LANES

    @pl.when(j == 0)
    def _():
        kv0b[0:blk, :] = jnp.zeros((blk, KV_WIDTH), BF16)
        kv1b[:, 0:blk, :] = jnp.zeros((DILATIONS[1], blk, KV_WIDTH), BF16)
        kv2b[:, 0:blk, :] = jnp.zeros((DILATIONS[2], blk, KV_WIDTH), BF16)

    @pl.when(j > 0)
    def _():
        kv0b[0:blk, :] = kv0b[tm:tm + blk, :]
        kv1b[:, 0:blk, :] = kv1b[:, blk:2 * blk, :]

    @pl.when((j > 0) & (p == 0))
    def _():
        kv2b[:, 0:blk, :] = kv2b[:, blk:2 * blk, :]

    h = _rmsnorm_rows(x_ref[0], g1_ref[...]).astype(BF16)
    for c in range(0, QKV_COLS // LANES, 2):
        res = _dot(h, w_ref[:, c * LANES:(c + 2) * LANES])
        if c < 3 * qpan:
            res = res * (ATTN_DIM ** -0.5)
        nat[c] = res[:, 0:LANES]
        nat[c + 1] = res[:, LANES:2 * LANES]

    for g, t_ref in enumerate((t0_ref, t1_ref, t2_ref)):
        keep = WINDOWS[g]
        kv_first = 3 * qpan + g * kvpan
        if keep >= tm:
            @pl.when(j >= nj - keep // tm)
            def _():
                for c in range(kvpan):
                    t_ref[0, :, c * LANES:(c + 1) * LANES] = nat[kv_first + c]
        else:
            @pl.when(j == nj - 1)
            def _():
                for c in range(kvpan):
                    t_ref[0, :, c * LANES:(c + 1) * LANES] = nat[kv_first + c, tm - keep:tm, :]

    for c in range(qpan):
        q0b[:, c * LANES:(c + 1) * LANES] = nat[c].astype(BF16)
    for c in range(kvpan):
        kv0b[blk:blk + tm, c * LANES:(c + 1) * LANES] = nat[3 * qpan + c].astype(BF16)
    d1 = DILATIONS[1]
    for r in range(d1):
        for c in range(qpan):
            q1b[r, :, c * LANES:(c + 1) * LANES] = nat[qpan + c, pl.ds(r, tm // d1, stride=d1), :].astype(BF16)
        for c in range(kvpan):
            kv1b[r, blk:2 * blk, c * LANES:(c + 1) * LANES] = (
                nat[3 * qpan + kvpan + c, pl.ds(r, tm // d1, stride=d1), :].astype(BF16))
    d2 = DILATIONS[2]
    sub = tm // d2
    row2 = pl.multiple_of(p * sub, sub)
    for r in range(d2):
        for c in range(qpan):
            q2b[r, pl.ds(row2, sub), c * LANES:(c + 1) * LANES] = (
                nat[2 * qpan + c, pl.ds(r, sub, stride=d2), :].astype(BF16))
        for c in range(kvpan):
            kv2b[r, pl.ds(blk + row2, sub), c * LANES:(c + 1) * LANES] = (
                nat[3 * qpan + 2 * kvpan + c, pl.ds(r, sub, stride=d2), :].astype(BF16))

    key_idx = lax.broadcasted_iota(jnp.int32, (blk, 2 * blk), 1)
    delta = lax.broadcasted_iota(jnp.int32, (blk, 2 * blk), 0) + blk - key_idx
    in_band = (delta >= 0) & (delta <= blk)
    own_key = key_idx >= blk
    base = pl.multiple_of(p * tm, tm)

    def store(group, rows, res):
        for pr, (o_pair, l_pair) in enumerate(res):
            o_nat[group, pr, rows, :] = o_pair
            l_nat[group, pr, rows, :] = l_pair

    for i in range(tm // blk):
        ok = in_band & (own_key | (j > 0)) if i == 0 else in_band
        res = _banded_block(0, q0b[i * blk:(i + 1) * blk, :], kv0b[i * blk:(i + 2) * blk, :], ok)
        store(0, pl.ds(base + i * blk, blk), res)

    ok1 = in_band & (own_key | (j > 0))
    for r in range(d1):
        store(1, pl.ds(base + r, blk, stride=d1), _banded_block(1, q1b[r], kv1b[r], ok1))

    @pl.when(p == SUB_STEPS - 1)
    def _():
        ok2 = in_band & (own_key | (j >= SUB_STEPS))
        unroll = 4

        def g2_body(it, carry):
            for k in range(unroll):
                r = it * unroll + k
                store(2, pl.ds(r, blk, stride=d2), _banded_block(2, q2b[r], kv2b[r], ok2))
            return carry

        lax.fori_loop(0, d2 // unroll, g2_body, 0)

        rb = 256
        for k in range(ATTN_SUPER // rb):
            rows = slice(k * rb, (k + 1) * rb)
            for pr in range(ATTN_HEADS // 2):
                l0, l1, l2 = l_nat[0, pr, rows, :], l_nat[1, pr, rows, :], l_nat[2, pr, rows, :]
                m = jnp.maximum(jnp.maximum(l0, l1), l2)
                e0, e1, e2 = jnp.exp(l0 - m), jnp.exp(l1 - m), jnp.exp(l2 - m)
                o = (e0 * o_nat[0, pr, rows, :] + e1 * o_nat[1, pr, rows, :]
                     + e2 * o_nat[2, pr, rows, :]) / (e0 + e1 + e2)
                o_ref[0, rows, pr * LANES:(pr + 1) * LANES] = o.astype(o_ref.dtype)


def _attn_prompt(x, g1, w_qkv):
    n, s, _ = x.shape
    tm = ATTN_TILE
    nj = s // tm
    assert s % ATTN_SUPER == 0
    const = lambda b, j: (0, 0)

    def tail_spec(keep):
        if keep >= tm:
            first = nj - keep // tm
            return pl.BlockSpec((1, tm, KV_WIDTH), lambda b, j: (b, jnp.maximum(j - first, 0), 0))
        return pl.BlockSpec((1, keep, KV_WIDTH), lambda b, j: (b, 0, 0))

    d1, d2 = DILATIONS[1], DILATIONS[2]
    pairs = ATTN_HEADS // 2
    return pl.pallas_call(
        _attn_prompt_kernel,
        grid=(n, nj),
        in_specs=[pl.BlockSpec((1, tm, D_MODEL), lambda b, j: (b, j, 0)), pl.BlockSpec((1, D_MODEL), const),
                  pl.BlockSpec((D_MODEL, QKV_COLS), const, pipeline_mode=pl.Buffered(1))],
        out_specs=[pl.BlockSpec((1, ATTN_SUPER, ATTN_WIDTH), lambda b, j: (b, j // SUB_STEPS, 0))]
        + [tail_spec(w) for w in WINDOWS],
        out_shape=[jax.ShapeDtypeStruct((n, s, ATTN_WIDTH), BF16)]
        + [jax.ShapeDtypeStruct((n, w, KV_WIDTH), F32) for w in WINDOWS],
        scratch_shapes=[
            pltpu.VMEM((QKV_COLS // LANES, tm, LANES), F32),
            pltpu.VMEM((tm, ATTN_WIDTH), BF16),
            pltpu.VMEM((ATTN_BLOCK + tm, KV_WIDTH), BF16),
            pltpu.VMEM((d1, tm // d1, ATTN_WIDTH), BF16),
            pltpu.VMEM((d1, ATTN_BLOCK + tm // d1, KV_WIDTH), BF16),
            pltpu.VMEM((d2, ATTN_SUPER // d2, ATTN_WIDTH), BF16),
            pltpu.VMEM((d2, ATTN_BLOCK + ATTN_SUPER // d2, KV_WIDTH), BF16),
            pltpu.VMEM((3, pairs, ATTN_SUPER, LANES), F32),
            pltpu.VMEM((3, pairs, ATTN_SUPER, LANES), F32),
        ],
        compiler_params=pltpu.CompilerParams(
            dimension_semantics=("parallel", "arbitrary"), vmem_limit_bytes=VMEM_LIMIT),
        name="attn_prompt",
    )(x, g1, w_qkv)


def _resident(shape):
    return pl.BlockSpec(shape, lambda *_: (0,) * len(shape), pipeline_mode=pl.Buffered(1))


def _merge_kernel(x_ref, yp_ref, z_ref, oa_ref, g1_ref, ng_ref, wg_ref, ws_ref, wa_ref, wo_ref, out_ref, ys_scr):
    xf = x_ref[...]
    h = _rmsnorm_rows(xf, g1_ref[...]).astype(BF16)
    gates = _dot(h, wg_ref[...])
    gates = 1.0 / (1.0 + jnp.exp(-gates))
    gw = D_INNER // N_GROUPS
    for g in range(N_GROUPS):
        cols = slice(g * gw, (g + 1) * gw)
        y = yp_ref[:, cols].astype(F32) * _silu(z_ref[:, cols].astype(F32))
        ys_scr[:, cols] = _rmsnorm_rows(y, ng_ref[:, cols]).astype(BF16)
    merged = (gates[:, 0:D_MODEL] * _dot(ys_scr[...], ws_ref[...])
              + gates[:, D_MODEL:2 * D_MODEL] * _dot(oa_ref[...].astype(BF16), wa_ref[...]))
    out_ref[...] = xf + _dot(merged.astype(BF16), wo_ref[...])


def _merge(x2d, y_pre, z, o_attn, g1, ng, w_gates, w_ssm_out, w_attn_out, w_o, *, tm):
    m = x2d.shape[0]
    row = lambda i: (i, 0)
    return pl.pallas_call(
        _merge_kernel,
        grid=(m // tm,),
        in_specs=[pl.BlockSpec((tm, D_MODEL), row), pl.BlockSpec((tm, D_INNER), row),
                  pl.BlockSpec((tm, D_INNER), row), pl.BlockSpec((tm, ATTN_WIDTH), row),
                  _resident((1, D_MODEL)), _resident((1, D_INNER)), _resident((D_MODEL, 2 * D_MODEL)),
                  _resident((D_INNER, D_MODEL)), _resident((ATTN_WIDTH, D_MODEL)),
                  _resident((D_MODEL, D_MODEL))],
        out_specs=pl.BlockSpec((tm, D_MODEL), row),
        out_shape=jax.ShapeDtypeStruct((m, D_MODEL), F32),
        scratch_shapes=[pltpu.VMEM((tm, D_INNER), BF16)],
        compiler_params=pltpu.CompilerParams(
            dimension_semantics=("parallel",), vmem_limit_bytes=VMEM_LIMIT),
        name="merge",
    )(x2d, y_pre, z, o_attn, g1, ng, w_gates, w_ssm_out, w_attn_out, w_o)


FF_CHUNK = 256


def _ffn_kernel(x_ref, g2_ref, gf_ref, wg_ref, wu_ref, wd_ref, out_ref, act_scr):
    xf = x_ref[...]
    h = _rmsnorm_rows(xf, g2_ref[...]).astype(BF16)
    for c in range(D_FF // FF_CHUNK):
        cols = slice(c * FF_CHUNK, (c + 1) * FF_CHUNK)
        gate = _dot(h, wg_ref[:, cols])
        act_scr[:, cols] = (_silu(gate) * _dot(h, wu_ref[:, cols])).astype(BF16)
    out_ref[...] = _rmsnorm_rows(xf + _dot(act_scr[...], wd_ref[...]), gf_ref[...])


def _ffn(x2d, g2, gf, w_gate, w_up, w_down, *, tm):
    m = x2d.shape[0]
    row = lambda i: (i, 0)
    return pl.pallas_call(
        _ffn_kernel,
        grid=(m // tm,),
        in_specs=[pl.BlockSpec((tm, D_MODEL), row), _resident((1, D_MODEL)), _resident((1, D_MODEL)),
                  _resident((D_MODEL, D_FF)), _resident((D_MODEL, D_FF)), _resident((D_FF, D_MODEL))],
        out_specs=pl.BlockSpec((tm, D_MODEL), row),
        out_shape=jax.ShapeDtypeStruct((m, D_MODEL), F32),
        scratch_shapes=[pltpu.VMEM((tm, D_FF), BF16)],
        compiler_params=pltpu.CompilerParams(
            dimension_semantics=("parallel",), vmem_limit_bytes=VMEM_LIMIT),
        name="ffn",
    )(x2d, g2, gf, w_gate, w_up, w_down)


SAMPLE_BLOCK = 8


def _ssd_sample_pre_kernel(x_ref, g1_ref, w_ref, cs_ref, cw_ref, cb_ref, dtb_ref, alog_ref, e_ref,
                           z_ref, xs_ref, b_ref, c_ref, xdt_t_ref, da_t_ref, conv_out_ref):
    h = _rmsnorm_rows(x_ref[...], g1_ref[...]).astype(BF16)
    z_ref[...] = _dot(h, w_ref[:, 0:D_INNER])
    xbc = _dot(h, w_ref[:, D_INNER:D_INNER + CONV_DIM])
    dt_raw = _dot(h, w_ref[:, D_INNER + CONV_DIM:W_SSD_COLS])
    acc = cb_ref[...] + cw_ref[CONV_W - 1:CONV_W, :] * xbc
    for k in range(CONV_W - 1):
        hist = cs_ref[k]
        acc = acc + cw_ref[k:k + 1, :] * hist
        if k > 0:
            conv_out_ref[k - 1] = hist
    conv_out_ref[CONV_W - 2] = xbc
    xc = _silu(acc)
    xs = xc[:, 0:D_INNER]
    xs_ref[...] = xs
    b_ref[...] = xc[:, D_INNER:D_INNER + N_GROUPS * D_STATE]
    c_ref[...] = xc[:, D_INNER + N_GROUPS * D_STATE:CONV_DIM]
    dt = _softplus(dt_raw + dtb_ref[...])
    da = jnp.exp(dt * (-jnp.exp(alog_ref[...])))
    xdt = xs * _dot_f32_lhs(dt, e_ref[...])
    da_e = _dot_f32_lhs(da, e_ref[...])
    nb = x_ref.shape[0]
    for k in range(D_INNER // nb):
        xdt_t_ref[k * nb:(k + 1) * nb, :] = xdt[:, k * nb:(k + 1) * nb].T
        da_t_ref[k * nb:(k + 1) * nb, :] = da_e[:, k * nb:(k + 1) * nb].T


def _ssd_sample_pre(x2d, conv_state_t, p):
    nb = x2d.shape[0]
    full = lambda shape: pl.BlockSpec(shape, lambda i: (0,) * len(shape))
    return pl.pallas_call(
        _ssd_sample_pre_kernel,
        grid=(1,),
        in_specs=[full((nb, D_MODEL)), full((1, D_MODEL)), full((D_MODEL, W_SSD_COLS)),
                  full((CONV_W - 1, nb, CONV_DIM)), full((CONV_W, CONV_DIM)), full((1, CONV_DIM)),
                  full((1, DT_PAD)), full((1, DT_PAD)), full((DT_PAD, D_INNER))],
        out_specs=[full((nb, D_INNER)), full((nb, D_INNER)), full((nb, N_GROUPS * D_STATE)),
                   full((nb, N_GROUPS * D_STATE)), full((D_INNER, nb)), full((D_INNER, nb)),
                   full((CONV_W - 1, nb, CONV_DIM))],
        out_shape=[jax.ShapeDtypeStruct((nb, D_INNER), F32), jax.ShapeDtypeStruct((nb, D_INNER), F32),
                   jax.ShapeDtypeStruct((nb, N_GROUPS * D_STATE), F32),
                   jax.ShapeDtypeStruct((nb, N_GROUPS * D_STATE), F32),
                   jax.ShapeDtypeStruct((D_INNER, nb), F32), jax.ShapeDtypeStruct((D_INNER, nb), F32),
                   jax.ShapeDtypeStruct((CONV_W - 1, nb, CONV_DIM), F32)],
        compiler_params=pltpu.CompilerParams(vmem_limit_bytes=VMEM_LIMIT),
        name="ssd_sample_pre",
    )(x2d, p["g1"], p["w_ssd"], conv_state_t, p["conv_w"], p["conv_b"], p["dtb"], p["alog"], p["expand"])


def _ssd_sample_state_kernel(st_ref, xdt_ref, da_ref, b_ref, c_ref, xs_ref, dskip_ref,
                             st_out_ref, y_ref, y_scr):
    grows = D_INNER // N_GROUPS
    for i in range(SAMPLE_BLOCK):
        for g in range(N_GROUPS):
            rows = slice(g * grows, (g + 1) * grows)
            s_new = (st_ref[i, rows, :] * da_ref[0, rows, i:i + 1]
                     + xdt_ref[0, rows, i:i + 1] * b_ref[i, g:g + 1, :])
            st_out_ref[i, rows, :] = s_new
            y_t = _dot_nt(c_ref[i].astype(BF16), s_new.astype(BF16))
            y_scr[i:i + 1, rows] = y_t[g:g + 1, :]
    y_ref[...] = (y_scr[...] + dskip_ref[...] * xs_ref[...]).astype(y_ref.dtype)


def _ssd_sample_state(state, xdt3, da3, b3, c3, xs, p):
    nb = state.shape[0]
    bb = SAMPLE_BLOCK
    const = lambda i: (0, 0)
    return pl.pallas_call(
        _ssd_sample_state_kernel,
        grid=(nb // bb,),
        in_specs=[pl.BlockSpec((bb, D_INNER, D_STATE), lambda i: (i, 0, 0)),
                  pl.BlockSpec((1, D_INNER, bb), lambda i: (i, 0, 0)),
                  pl.BlockSpec((1, D_INNER, bb), lambda i: (i, 0, 0)),
                  pl.BlockSpec((bb, N_GROUPS, D_STATE), lambda i: (i, 0, 0)),
                  pl.BlockSpec((bb, N_GROUPS, D_STATE), lambda i: (i, 0, 0)),
                  pl.BlockSpec((bb, D_INNER), lambda i: (i, 0)),
                  pl.BlockSpec((1, D_INNER), const)],
        out_specs=[pl.BlockSpec((bb, D_INNER, D_STATE), lambda i: (i, 0, 0)),
                   pl.BlockSpec((bb, D_INNER), lambda i: (i, 0))],
        out_shape=[jax.ShapeDtypeStruct((nb, D_INNER, D_STATE), F32),
                   jax.ShapeDtypeStruct((nb, D_INNER), BF16)],
        scratch_shapes=[pltpu.VMEM((bb, D_INNER), F32)],
        compiler_params=pltpu.CompilerParams(
            dimension_semantics=("parallel",), vmem_limit_bytes=VMEM_LIMIT),
        name="ssd_sample_state",
    )(state, xdt3, da3, b3, c3, xs, p["dskip_e"])


ATTN_SAMPLE_BLOCK = 2


def _proj_rows_kernel(x_ref, g1_ref, w_ref, wt_ref, out_ref, out_t_ref):
    h = _rmsnorm_rows(x_ref[...], g1_ref[...]).astype(BF16)
    out_ref[...] = _dot(h, w_ref[...])
    out_t_ref[...] = _dot_nt(wt_ref[...], h)


def _proj_rows(x2d, g1, w, w_t):
    nb, cols = x2d.shape[0], w.shape[1]
    full = lambda shape: pl.BlockSpec(shape, lambda i: (0,) * len(shape))
    return pl.pallas_call(
        _proj_rows_kernel,
        grid=(1,),
        in_specs=[full((nb, D_MODEL)), full((1, D_MODEL)), full((D_MODEL, cols)), full((cols, D_MODEL))],
        out_specs=[full((nb, cols)), full((cols, nb))],
        out_shape=[jax.ShapeDtypeStruct((nb, cols), F32), jax.ShapeDtypeStruct((cols, nb), F32)],
        compiler_params=pltpu.CompilerParams(vmem_limit_bytes=VMEM_LIMIT),
        name="proj_rows",
    )(x2d, g1, w, w_t)


def _attn_sample_kernel(qkv_t_ref, c0_ref, c1_ref, c2_ref, o_t_ref):
    kv0 = 3 * ATTN_WIDTH
    for i in range(ATTN_SAMPLE_BLOCK):
        lane = slice(i, i + 1)
        for h in range(ATTN_HEADS):
            hrows = slice(h * ATTN_DIM, (h + 1) * ATTN_DIM)
            scores, new_scores = [], []
            for g, c_ref in enumerate((c0_ref, c1_ref, c2_ref)):
                w, dil = WINDOWS[g], DILATIONS[g]
                q_col = qkv_t_ref[0, g * ATTN_WIDTH + h * ATTN_DIM:g * ATTN_WIDTH + (h + 1) * ATTN_DIM, lane]
                q_col = q_col * (ATTN_DIM ** -0.5)
                k_lo = kv0 + g * KV_WIDTH + h * ATTN_DIM
                k_col = qkv_t_ref[0, k_lo:k_lo + ATTN_DIM, lane]
                s = jnp.sum(c_ref[i, hrows, :] * q_col, axis=0, keepdims=True)
                pos = lax.broadcasted_iota(jnp.int32, (1, w), 1)
                on_stride = (pos & (dil - 1)) == 0
                s = jnp.where(on_stride, s - _alibi_slope(g, h) * (w - pos).astype(F32), NEG_BIG)
                scores.append(s)
                new_scores.append(jnp.sum(q_col * k_col, axis=0, keepdims=True))
            m = new_scores[0]
            for g in range(3):
                m = jnp.maximum(m, jnp.maximum(new_scores[g], jnp.max(scores[g], axis=-1, keepdims=True)))
            den = jnp.zeros((1, 1), F32)
            acc = jnp.zeros((ATTN_DIM, 1), F32)
            for g, c_ref in enumerate((c0_ref, c1_ref, c2_ref)):
                p = jnp.exp(scores[g] - m)
                p_new = jnp.exp(new_scores[g] - m)
                den = den + jnp.sum(p, axis=-1, keepdims=True) + p_new
                v_lo = kv0 + g * KV_WIDTH + ATTN_WIDTH + h * ATTN_DIM
                v_col = qkv_t_ref[0, v_lo:v_lo + ATTN_DIM, lane]
                v_t = c_ref[i, ATTN_WIDTH + h * ATTN_DIM:ATTN_WIDTH + (h + 1) * ATTN_DIM, :]
                acc = acc + jnp.sum(v_t * p, axis=-1, keepdims=True) + p_new * v_col
            o_t_ref[0, hrows, lane] = acc / den


def _attn_sample(qkv_t, caches):
    nb = qkv_t.shape[1]
    bb = ATTN_SAMPLE_BLOCK
    views = []
    for g, c in enumerate(caches):
        assert c.shape[1] == WINDOWS[g]
        views.append(jnp.transpose(c, (0, 2, 3, 4, 1)).reshape(nb, KV_WIDTH, WINDOWS[g]))
    qkv_blocks = qkv_t.reshape(QKV_COLS, nb // bb, bb).transpose(1, 0, 2)
    o_t = pl.pallas_call(
        _attn_sample_kernel,
        grid=(nb // bb,),
        in_specs=[pl.BlockSpec((1, QKV_COLS, bb), lambda i: (i, 0, 0))]
        + [pl.BlockSpec((bb, KV_WIDTH, w), lambda i: (i, 0, 0)) for w in WINDOWS],
        out_specs=pl.BlockSpec((1, ATTN_WIDTH, bb), lambda i: (i, 0, 0)),
        out_shape=jax.ShapeDtypeStruct((nb // bb, ATTN_WIDTH, bb), F32),
        compiler_params=pltpu.CompilerParams(
            dimension_semantics=("parallel",), vmem_limit_bytes=VMEM_LIMIT),
        name="attn_sample",
    )(qkv_blocks, *views)
    return o_t.transpose(0, 2, 1).reshape(nb, ATTN_WIDTH)


def _layer_sample(x2d, state_ssm, state_conv, caches, wp):
    nb = x2d.shape[0]
    p = wp["ssd"]
    bb = SAMPLE_BLOCK
    z, xs, bm, cm, xdt_t, da_t, conv_new = _ssd_sample_pre(
        x2d, jnp.transpose(state_conv, (1, 0, 2)), p)
    to_blocks = lambda t: t.reshape(D_INNER, nb // bb, bb).transpose(1, 0, 2)
    st_new, y_pre = _ssd_sample_state(
        state_ssm.reshape(nb, D_INNER, D_STATE), to_blocks(xdt_t), to_blocks(da_t),
        bm.reshape(nb, N_GROUPS, D_STATE), cm.reshape(nb, N_GROUPS, D_STATE), xs, p)
    qkv, qkv_t = _proj_rows(x2d, p["g1"], wp["w_qkv"], wp["w_qkv"].T)
    kv_new = qkv[:, 3 * ATTN_WIDTH:]
    o = _attn_sample(qkv_t, caches)
    x1 = _merge(x2d, y_pre, z, o, p["g1"], wp["ng"], wp["w_gates"], wp["w_ssm_out"], wp["w_attn_out"],
                wp["w_o"], tm=nb)
    y = _ffn(x1, wp["g2"], wp["gf"], wp["w_ffn_gate"], wp["w_ffn_up"], wp["w_ffn_down"], tm=nb)
    return y, st_new, conv_new, kv_new


def _prep_attn_weight(w_in):
    cols = [w_in[:, OFF_Q:OFF_Q + 3 * ATTN_WIDTH]]
    for g in range(3):
        cols.append(w_in[:, OFF_K + g * ATTN_WIDTH:OFF_K + (g + 1) * ATTN_WIDTH])
        cols.append(w_in[:, OFF_V + g * ATTN_WIDTH:OFF_V + (g + 1) * ATTN_WIDTH])
    return jnp.concatenate(cols, axis=1).astype(BF16)


def _layer_prompt(x, wp):
    n, s, _ = x.shape
    y_pre, z, st, conv = _ssd_prompt(x, **wp["ssd"])
    o_attn, t0, t1, t2 = _attn_prompt(x, wp["ssd"]["g1"], wp["w_qkv"])
    x1 = _merge(x.reshape(n * s, D_MODEL), y_pre.reshape(n * s, D_INNER), z.reshape(n * s, D_INNER),
                o_attn.reshape(n * s, ATTN_WIDTH), wp["ssd"]["g1"], wp["ng"],
                wp["w_gates"], wp["w_ssm_out"], wp["w_attn_out"], wp["w_o"], tm=512)
    y = _ffn(x1, wp["g2"], wp["gf"], wp["w_ffn_gate"], wp["w_ffn_up"], wp["w_ffn_down"], tm=512)
    return y.reshape(n, s, D_MODEL), st, conv, (t0, t1, t2)


def _prep_weights(norm1_g, w_in, conv_w, conv_b, dt_bias, a_log, d_skip, ssm_norm_g, w_ssm_out, w_attn_out,
                  w_o, norm2_g, w_ffn_gate, w_ffn_up, w_ffn_down, norm_f_g):
    return dict(
        ssd=_prep_ssd_params(norm1_g, w_in, conv_w, conv_b, dt_bias, a_log, d_skip),
        w_qkv=_prep_attn_weight(w_in), ng=ssm_norm_g.reshape(1, D_INNER),
        w_gates=w_in[:, OFF_GS:OFF_GS + 2 * D_MODEL].astype(BF16),
        w_ssm_out=w_ssm_out.astype(BF16), w_attn_out=w_attn_out.astype(BF16), w_o=w_o.astype(BF16),
        g2=norm2_g.reshape(1, D_MODEL), gf=norm_f_g.reshape(1, D_MODEL),
        w_ffn_gate=w_ffn_gate.astype(BF16), w_ffn_up=w_ffn_up.astype(BF16),
        w_ffn_down=w_ffn_down.astype(BF16))


def kernel(x_prompt, x_sample, state_ssm, state_conv, cache_kv_w128, cache_kv_w512, cache_kv_w2048, norm1_g, w_in, conv_w, conv_b, dt_bias, a_log, d_skip, ssm_norm_g, w_ssm_out, w_attn_out, w_o, norm2_g, w_ffn_gate, w_ffn_up, w_ffn_down, norm_f_g):
    wp = _prep_weights(norm1_g[0], w_in[0], conv_w[0], conv_b[0], dt_bias[0], a_log[0], d_skip[0],
                       ssm_norm_g[0], w_ssm_out[0], w_attn_out[0], w_o[0], norm2_g[0], w_ffn_gate[0],
                       w_ffn_up[0], w_ffn_down[0], norm_f_g)
    n, s, _ = x_prompt.shape
    nb = x_sample.shape[0]
    y_p, st_p, conv_p, tails = _layer_prompt(x_prompt, wp)
    y_s, st_s, conv_s, kv_s = _layer_sample(
        x_sample.reshape(nb, D_MODEL), state_ssm[0], state_conv[0],
        (cache_kv_w128[0], cache_kv_w512[0], cache_kv_w2048[0]), wp)
    kv_tail_shape = (2, ATTN_HEADS, ATTN_DIM)
    outs = [y_p, y_s.reshape(nb, 1, D_MODEL),
            st_p.reshape(1, n, N_HEADS, HEAD_DIM, D_STATE), st_s.reshape(1, nb, N_HEADS, HEAD_DIM, D_STATE),
            conv_p.reshape(1, n, CONV_W - 1, CONV_DIM), jnp.transpose(conv_s, (1, 0, 2)).reshape(1, nb, CONV_W - 1, CONV_DIM)]
    for g in range(3):
        outs.append(tails[g].reshape((1, n, WINDOWS[g]) + kv_tail_shape))
        outs.append(kv_s[:, g * KV_WIDTH:(g + 1) * KV_WIDTH].reshape((1, nb, 1) + kv_tail_shape))
    return tuple(outs)
```

```python
import functools

import jax
import jax.numpy as jnp
from jax import lax
from jax.experimental import pallas as pl
from jax.experimental.pallas import tpu as pltpu

F32 = jnp.float32
BF16 = jnp.bfloat16

D_MODEL = 1024
D_INNER = 2048
HEAD_DIM = 64
N_HEADS = 32
N_GROUPS = 4
HEADS_PER_SSM_GROUP = N_HEADS // N_GROUPS
D_STATE = 128
CONV_W = 4
CONV_DIM = D_INNER + 2 * N_GROUPS * D_STATE
CHUNK = 128
DT_PAD = 128
ATTN_HEADS = 4
ATTN_DIM = 64
ATTN_WIDTH = ATTN_HEADS * ATTN_DIM
WINDOWS = (128, 512, 2048)
DILATIONS = (1, 4, 16)
ATTN_BLOCK = 128
D_FF = 2816
RMS_EPS = 1e-6
NEG_BIG = -1e30
OFF_XBC = D_INNER
OFF_DT = OFF_XBC + CONV_DIM
OFF_Q = OFF_DT + N_HEADS
OFF_K = OFF_Q + 3 * ATTN_WIDTH
OFF_V = OFF_K + 3 * ATTN_WIDTH
OFF_GS = OFF_V + 3 * ATTN_WIDTH
W_SSD_COLS = D_INNER + CONV_DIM + DT_PAD
VMEM_LIMIT = 52 * 1024 * 1024


def _dot(a, b):
    return jnp.dot(a, b, preferred_element_type=F32)


def _dot_nt(a, b):
    return lax.dot_general(a, b, (((1,), (1,)), ((), ())), preferred_element_type=F32)


def _split3(v):
    hi = v.astype(BF16)
    r1 = v - hi.astype(F32)
    mid = r1.astype(BF16)
    lo = (r1 - mid.astype(F32)).astype(BF16)
    return hi, mid, lo


def _dot_f32_rhs(a_bf16_exact, v):
    hi, mid, lo = _split3(v)
    return _dot(a_bf16_exact, hi) + _dot(a_bf16_exact, mid) + _dot(a_bf16_exact, lo)


def _dot_f32_lhs(v, b_bf16_exact):
    hi, mid, lo = _split3(v)
    return _dot(hi, b_bf16_exact) + _dot(mid, b_bf16_exact) + _dot(lo, b_bf16_exact)


def _rmsnorm_rows(xf, g_row):
    ms = jnp.mean(xf * xf, axis=-1, keepdims=True)
    return xf * lax.rsqrt(ms + RMS_EPS) * g_row


def _silu(v):
    h = 0.5 * v
    return h + h * jnp.tanh(h)


def _log1p(e):
    u = 1.0 + e
    return jnp.where(u == 1.0, e, jnp.log(u) * (e / (u - 1.0)))


def _softplus(v):
    return jnp.maximum(v, 0.0) + _log1p(jnp.exp(-jnp.abs(v)))


def _ssd_chunk(out_rows, xc_ref, xslot, dt_ref, dslot, st_ref, y_ref, e_ref, dtb_row, a_row, dskip_row,
               fillers):
    tri_r = lax.broadcasted_iota(jnp.int32, (CHUNK, CHUNK), 0)
    tri_c = lax.broadcasted_iota(jnp.int32, (CHUNK, CHUNK), 1)
    causal = tri_r >= tri_c
    ltri = causal.astype(BF16)
    lane_lo = tri_c < HEAD_DIM
    fillers = list(fillers)
    n_slots = N_HEADS // 2

    def run_fillers(slots_left):
        for _ in range(-(-len(fillers) // slots_left)):
            fillers.pop(0)()

    dt = _softplus(dt_ref[dslot] + dtb_row)
    adt = dt * a_row
    acum = _dot_f32_rhs(ltri, adt)
    acum_dt_t = (acum - jnp.log(dt)).T
    a_last = acum[CHUNK - 1:CHUNK, :]
    w_t = (dt * jnp.exp(a_last - acum)).T
    cdec = jnp.exp(a_last)
    cdec_e = _dot_f32_lhs(jnp.broadcast_to(cdec, (8, DT_PAD)), e_ref[...])[0:1, :]

    for g in range(N_GROUPS):
        b_g = xc_ref[xslot, :, D_INNER + g * D_STATE:D_INNER + (g + 1) * D_STATE]
        c_g = xc_ref[xslot, :, D_INNER + N_GROUPS * D_STATE + g * D_STATE:
                     D_INNER + N_GROUPS * D_STATE + (g + 1) * D_STATE]
        b_bf = b_g.astype(BF16)
        c_bf = c_g.astype(BF16)
        cb = _dot_nt(c_bf, b_bf)
        b_t = b_g.T
        gcols = slice(g * HEADS_PER_SSM_GROUP * HEAD_DIM, (g + 1) * HEADS_PER_SSM_GROUP * HEAD_DIM)
        y_off = _dot(c_bf, st_ref[:, gcols].astype(BF16))
        for pr in range(HEADS_PER_SSM_GROUP // 2):
            h0 = g * HEADS_PER_SSM_GROUP + 2 * pr
            pcols = slice(h0 * HEAD_DIM, (h0 + 2) * HEAD_DIM)
            xs_f32 = xc_ref[xslot, :, pcols]
            xs_pair = xs_f32.astype(BF16)
            xs_diag = jnp.concatenate([jnp.where(lane_lo, xs_pair, jnp.zeros_like(xs_pair)),
                                       jnp.where(lane_lo, jnp.zeros_like(xs_pair), xs_pair)], axis=0)
            a_b, wmats, smats = [], [], []
            for k in range(2):
                h = h0 + k
                ab = jnp.broadcast_to(acum[:, h:h + 1], (CHUNK, CHUNK))
                a_b.append(ab)
                dec = jnp.exp(jnp.where(causal, ab - acum_dt_t[h:h + 1, :], NEG_BIG))
                wmats.append((cb * dec).astype(BF16))
                smats.append((b_t * w_t[h:h + 1, :]).astype(BF16))
            y_pair = _dot(jnp.concatenate(wmats, axis=1), xs_diag)
            s_pair = _dot(jnp.concatenate(smats, axis=1), xs_diag)
            ea_pair = jnp.exp(jnp.where(lane_lo, a_b[0], a_b[1]))
            lo = (2 * pr) * HEAD_DIM
            y_pre = y_pair + y_off[:, lo:lo + 2 * HEAD_DIM] * ea_pair + dskip_row[:, pcols] * xs_f32
            y_ref[0, out_rows, pcols] = y_pre.astype(y_ref.dtype)
            st_ref[:, pcols] = st_ref[:, pcols] * cdec_e[:, pcols] + s_pair
            run_fillers(n_slots - (g * (HEADS_PER_SSM_GROUP // 2) + pr))
    assert not fillers


def _ssd_prompt_kernel(x_ref, g1_ref, w_ref, cw_ref, cb_ref, dtb_ref, alog_ref, dskip_ref,
                       e_ref, y_ref, z_out_ref, st_out_ref, conv_out_ref,
                       ext_ref, xc_ref, dt_ref, st_ref, h_scr, *, tl):
    j = pl.program_id(1)
    nj = pl.num_programs(1)
    nchunk = tl // CHUNK
    hist = 8

    @pl.when(j == 0)
    def _():
        st_ref[...] = jnp.zeros_like(st_ref)
        ext_ref[1, CHUNK:CHUNK + hist, :] = jnp.zeros((hist, CONV_DIM), F32)

    dtb_row = dtb_ref[...]
    a_row = -jnp.exp(alog_ref[...])
    dskip_row = dskip_ref[...]
    piece = 512
    pieces = ([("z", lo, lo + piece) for lo in range(0, D_INNER, piece)]
              + [("xbc", lo, lo + piece) for lo in range(0, CONV_DIM, piece)] + [("dt", 0, DT_PAD)])
    cblk = 256

    def rows_of(k):
        return pl.ds(pl.multiple_of(k * CHUNK, CHUNK), CHUNK)

    def project_thunks(k, slot):
        rows = rows_of(k)

        def normalize():
            h_scr[...] = _rmsnorm_rows(x_ref[0, rows, :], g1_ref[...]).astype(BF16)
            ext_ref[slot, 0:hist, :] = ext_ref[1 - slot, CHUNK:CHUNK + hist, :]

        def piece_thunk(kind, lo, hi):
            def run():
                if kind == "z":
                    z_out_ref[0, rows, lo:hi] = _dot(h_scr[...], w_ref[:, lo:hi]).astype(z_out_ref.dtype)
                elif kind == "xbc":
                    ext_ref[slot, hist:hist + CHUNK, lo:hi] = _dot(
                        h_scr[...], w_ref[:, D_INNER + lo:D_INNER + hi])
                else:
                    dt_ref[slot] = _dot(h_scr[...], w_ref[:, D_INNER + CONV_DIM:W_SSD_COLS])
            return run

        return [normalize] + [piece_thunk(*p) for p in pieces]

    def conv_thunks(slot):
        def block_thunk(cbk):
            def run():
                cols = slice(cbk * cblk, (cbk + 1) * cblk)
                win = ext_ref[slot, :, cols]
                acc = cb_ref[:, cols] + cw_ref[CONV_W - 1:CONV_W, cols] * win[hist:, :]
                for t in range(1, CONV_W):
                    tap = cw_ref[CONV_W - 1 - t:CONV_W - t, cols]
                    acc = acc + tap * pltpu.roll(win, t, axis=0)[hist:, :]
                xc_ref[slot, :, cols] = _silu(acc)
            return run

        return [block_thunk(c) for c in range(CONV_DIM // cblk)]

    def interleave(a, b):
        out = []
        for i in range(max(len(a), len(b))):
            out += a[i:i + 1] + b[i:i + 1]
        return out

    def scan(k, slot, fillers):
        _ssd_chunk(rows_of(k), xc_ref, slot, dt_ref, slot, st_ref, y_ref, e_ref,
                   dtb_row, a_row, dskip_row, fillers)

    for thunk in project_thunks(0, 0) + interleave(conv_thunks(0), project_thunks(1, 1)):
        thunk()

    def body(m, carry):
        for e in range(2):
            k = 2 * m + e
            scan(k, e, interleave(conv_thunks(1 - e), project_thunks(k + 2, e)))
        return carry

    lax.fori_loop(0, (nchunk - 2) // 2, body, 0)
    scan(nchunk - 2, 0, conv_thunks(1))
    scan(nchunk - 1, 1, [])

    @pl.when(j == nj - 1)
    def _():
        last = (nchunk - 1) % 2
        conv_out_ref[0] = ext_ref[last, hist + CHUNK - (CONV_W - 1):hist + CHUNK, :]
        for k in range(D_INNER // D_STATE):
            st_out_ref[0, k * D_STATE:(k + 1) * D_STATE, :] = st_ref[:, k * D_STATE:(k + 1) * D_STATE].T


def _ssd_prompt(x, g1, w_ssd, conv_w, conv_b, dtb, alog, dskip_e, expand, *, tl=1024):
    n, s, _ = x.shape
    assert (tl // CHUNK) % 2 == 0 and s % tl == 0
    kern = functools.partial(_ssd_prompt_kernel, tl=tl)
    const = lambda b, j: (0, 0)
    return pl.pallas_call(
        kern,
        grid=(n, s // tl),
        in_specs=[
            pl.BlockSpec((1, tl, D_MODEL), lambda b, j: (b, j, 0)),
            pl.BlockSpec((1, D_MODEL), const),
            pl.BlockSpec((D_MODEL, W_SSD_COLS), const, pipeline_mode=pl.Buffered(1)),
            pl.BlockSpec((CONV_W, CONV_DIM), const),
            pl.BlockSpec((1, CONV_DIM), const),
            pl.BlockSpec((1, DT_PAD), const),
            pl.BlockSpec((1, DT_PAD), const),
            pl.BlockSpec((1, D_INNER), const),
            pl.BlockSpec((DT_PAD, D_INNER), const),
        ],
        out_specs=[
            pl.BlockSpec((1, tl, D_INNER), lambda b, j: (b, j, 0)),
            pl.BlockSpec((1, tl, D_INNER), lambda b, j: (b, j, 0)),
            pl.BlockSpec((1, D_INNER, D_STATE), lambda b, j: (b, 0, 0)),
            pl.BlockSpec((1, CONV_W - 1, CONV_DIM), lambda b, j: (b, 0, 0)),
        ],
        out_shape=[
            jax.ShapeDtypeStruct((n, s, D_INNER), BF16),
            jax.ShapeDtypeStruct((n, s, D_INNER), BF16),
            jax.ShapeDtypeStruct((n, D_INNER, D_STATE), F32),
            jax.ShapeDtypeStruct((n, CONV_W - 1, CONV_DIM), F32),
        ],
        scratch_shapes=[
            pltpu.VMEM((2, CHUNK + 8, CONV_DIM), F32),
            pltpu.VMEM((2, CHUNK, CONV_DIM), F32),
            pltpu.VMEM((2, CHUNK, DT_PAD), F32),
            pltpu.VMEM((D_STATE, D_INNER), F32),
            pltpu.VMEM((CHUNK, D_MODEL), BF16),
        ],
        compiler_params=pltpu.CompilerParams(
            dimension_semantics=("parallel", "arbitrary"), vmem_limit_bytes=VMEM_LIMIT),
        name="ssd_prompt",
    )(x, g1, w_ssd, conv_w, conv_b, dtb, alog, dskip_e, expand)


def _prep_ssd_params(norm1_g, w_in, conv_w, conv_b, dt_bias, a_log, d_skip):
    w_ssd = jnp.concatenate(
        [w_in[:, 0:OFF_DT], w_in[:, OFF_DT:OFF_Q], jnp.zeros((D_MODEL, DT_PAD - N_HEADS), w_in.dtype)],
        axis=1).astype(BF16)
    pad = lambda v: jnp.pad(v, (0, DT_PAD - N_HEADS)).reshape(1, DT_PAD)
    expand = (jnp.arange(DT_PAD)[:, None] == (jnp.arange(D_INNER) // HEAD_DIM)[None, :]).astype(BF16)
    return dict(
        g1=norm1_g.reshape(1, D_MODEL), w_ssd=w_ssd, conv_w=conv_w, conv_b=conv_b.reshape(1, CONV_DIM),
        dtb=pad(dt_bias), alog=pad(a_log), dskip_e=jnp.repeat(d_skip, HEAD_DIM).reshape(1, D_INNER),
        expand=expand)


QKV_COLS = 9 * ATTN_WIDTH
KV_WIDTH = 2 * ATTN_WIDTH
LANES = 128
ATTN_TILE = 512
ATTN_SUPER = WINDOWS[2]
SUB_STEPS = ATTN_SUPER // ATTN_TILE


def _alibi_slope(group, head):
    return 2.0 ** (-8.0 * (group * ATTN_HEADS + head + 1) / (3 * ATTN_HEADS))


def _banded_block(group, q, kv, ok):
    dil = DILATIONS[group]
    blk = ATTN_BLOCK
    delta = (lax.broadcasted_iota(jnp.int32, (blk, 2 * blk), 0) + blk
             - lax.broadcasted_iota(jnp.int32, (blk, 2 * blk), 1))
    dist = (dil * delta).astype(F32)
    lane_lo_q = lax.broadcasted_iota(jnp.int32, (blk, 2 * ATTN_DIM), 1) < ATTN_DIM
    lane_lo_v = lax.broadcasted_iota(jnp.int32, (2 * blk, 2 * ATTN_DIM), 1) < ATTN_DIM
    out = []
    for pr in range(ATTN_HEADS // 2):
        cols = slice(pr * 2 * ATTN_DIM, (pr + 1) * 2 * ATTN_DIM)
        vcols = slice(ATTN_WIDTH + pr * 2 * ATTN_DIM, ATTN_WIDTH + (pr + 1) * 2 * ATTN_DIM)
        q_pair, k_pair, v_pair = q[:, cols], kv[:, cols], kv[:, vcols]
        o_pair = None
        lse = []
        for k in range(2):
            sel_q = lane_lo_q if k == 0 else ~lane_lo_q
            sel_v = lane_lo_v if k == 0 else ~lane_lo_v
            q_h = jnp.where(sel_q, q_pair, jnp.zeros_like(q_pair))
            v_h = jnp.where(sel_v, v_pair, jnp.zeros_like(v_pair))
            s = _dot_nt(q_h, k_pair)
            s = jnp.where(ok, s - _alibi_slope(group, 2 * pr + k) * dist, NEG_BIG)
            m = jnp.max(s, axis=-1, keepdims=True)
            p = jnp.exp(s - m)
            den = jnp.sum(p, axis=-1, keepdims=True)
            o_h = _dot(p.astype(BF16), v_h) / den
            o_pair = o_h if o_pair is None else o_pair + o_h
            lse.append(m + jnp.log(den))
        out.append((o_pair, jnp.where(lane_lo_q, lse[0], lse[1])))
    return out


def _attn_prompt_kernel(x_ref, g1_ref, w_ref, o_ref, t0_ref, t1_ref, t2_ref,
                        nat, q0b, kv0b, q1b, kv1b, q2b, kv2b, o_nat, l_nat):
    j = pl.program_id(1)
    nj = pl.num_programs(1)
    p = j % SUB_STEPS
    blk = ATTN_BLOCK
    tm = ATTN_TILE
    qpan = ATTN_WIDTH // LANES
    kvpan = KV_WIDTH // LANES

    @pl.when(j == 0)
    def _():
        kv0b[0:blk, :] = jnp.zeros((blk, KV_WIDTH), BF16)
        kv1b[:, 0:blk, :] = jnp.zeros((DILATIONS[1], blk, KV_WIDTH), BF16)
        kv2b[:, 0:blk, :] = jnp.zeros((DILATIONS[2], blk, KV_WIDTH), BF16)

    @pl.when(j > 0)
    def _():
        kv0b[0:blk, :] = kv0b[tm:tm + blk, :]
        kv1b[:, 0:blk, :] = kv1b[:, blk:2 * blk, :]

    @pl.when((j > 0) & (p == 0))
    def _():
        kv2b[:, 0:blk, :] = kv2b[:, blk:2 * blk, :]

    h = _rmsnorm_rows(x_ref[0], g1_ref[...]).astype(BF16)
    for c in range(0, QKV_COLS // LANES, 2):
        res = _dot(h, w_ref[:, c * LANES:(c + 2) * LANES])
        if c < 3 * qpan:
            res = res * (ATTN_DIM ** -0.5)
        nat[c] = res[:, 0:LANES]
        nat[c + 1] = res[:, LANES:2 * LANES]

    for g, t_ref in enumerate((t0_ref, t1_ref, t2_ref)):
        keep = WINDOWS[g]
        kv_first = 3 * qpan + g * kvpan
        if keep >= tm:
            @pl.when(j >= nj - keep // tm)
            def _():
                for c in range(kvpan):
                    t_ref[0, :, c * LANES:(c + 1) * LANES] = nat[kv_first + c]
        else:
            @pl.when(j == nj - 1)
            def _():
                for c in range(kvpan):
                    t_ref[0, :, c * LANES:(c + 1) * LANES] = nat[kv_first + c, tm - keep:tm, :]

    for c in range(qpan):
        q0b[:, c * LANES:(c + 1) * LANES] = nat[c].astype(BF16)
    for c in range(kvpan):
        kv0b[blk:blk + tm, c * LANES:(c + 1) * LANES] = nat[3 * qpan + c].astype(BF16)
    d1 = DILATIONS[1]
    for r in range(d1):
        for c in range(qpan):
            q1b[r, :, c * LANES:(c + 1) * LANES] = nat[qpan + c, pl.ds(r, tm // d1, stride=d1), :].astype(BF16)
        for c in range(kvpan):
            kv1b[r, blk:2 * blk, c * LANES:(c + 1) * LANES] = (
                nat[3 * qpan + kvpan + c, pl.ds(r, tm // d1, stride=d1), :].astype(BF16))
    d2 = DILATIONS[2]
    sub = tm // d2
    row2 = pl.multiple_of(p * sub, sub)
    for r in range(d2):
        for c in range(qpan):
            q2b[r, pl.ds(row2, sub), c * LANES:(c + 1) * LANES] = (
                nat[2 * qpan + c, pl.ds(r, sub, stride=d2), :].astype(BF16))
        for c in range(kvpan):
            kv2b[r, pl.ds(blk + row2, sub), c * LANES:(c + 1) * LANES] = (
                nat[3 * qpan + 2 * kvpan + c, pl.ds(r, sub, stride=d2), :].astype(BF16))

    key_idx = lax.broadcasted_iota(jnp.int32, (blk, 2 * blk), 1)
    delta = lax.broadcasted_iota(jnp.int32, (blk, 2 * blk), 0) + blk - key_idx
    in_band = (delta >= 0) & (delta <= blk)
    own_key = key_idx >= blk
    base = pl.multiple_of(p * tm, tm)

    def store(group, rows, res):
        for pr, (o_pair, l_pair) in enumerate(res):
            o_nat[group, pr, rows, :] = o_pair
            l_nat[group, pr, rows, :] = l_pair

    for i in range(tm // blk):
        ok = in_band & (own_key | (j > 0)) if i == 0 else in_band
        res = _banded_block(0, q0b[i * blk:(i + 1) * blk, :], kv0b[i * blk:(i + 2) * blk, :], ok)
        store(0, pl.ds(base + i * blk, blk), res)

    ok1 = in_band & (own_key | (j > 0))
    for r in range(d1):
        store(1, pl.ds(base + r, blk, stride=d1), _banded_block(1, q1b[r], kv1b[r], ok1))

    @pl.when(p == SUB_STEPS - 1)
    def _():
        ok2 = in_band & (own_key | (j >= SUB_STEPS))
        unroll = 4

        def g2_body(it, carry):
            for k in range(unroll):
                r = it * unroll + k
                store(2, pl.ds(r, blk, stride=d2), _banded_block(2, q2b[r], kv2b[r], ok2))
            return carry

        lax.fori_loop(0, d2 // unroll, g2_body, 0)

        rb = 256
        for k in range(ATTN_SUPER // rb):
            rows = slice(k * rb, (k + 1) * rb)
            for pr in range(ATTN_HEADS // 2):
                l0, l1, l2 = l_nat[0, pr, rows, :], l_nat[1, pr, rows, :], l_nat[2, pr, rows, :]
                m = jnp.maximum(jnp.maximum(l0, l1), l2)
                e0, e1, e2 = jnp.exp(l0 - m), jnp.exp(l1 - m), jnp.exp(l2 - m)
                o = (e0 * o_nat[0, pr, rows, :] + e1 * o_nat[1, pr, rows, :]
                     + e2 * o_nat[2, pr, rows, :]) / (e0 + e1 + e2)
                o_ref[0, rows, pr * LANES:(pr + 1) * LANES] = o.astype(o_ref.dtype)


def _attn_prompt(x, g1, w_qkv):
    n, s, _ = x.shape
    tm = ATTN_TILE
    nj = s // tm
    assert s % ATTN_SUPER == 0
    const = lambda b, j: (0, 0)

    def tail_spec(keep):
        if keep >= tm:
            first = nj - keep // tm
            return pl.BlockSpec((1, tm, KV_WIDTH), lambda b, j: (b, jnp.maximum(j - first, 0), 0))
        return pl.BlockSpec((1, keep, KV_WIDTH), lambda b, j: (b, 0, 0))

    d1, d2 = DILATIONS[1], DILATIONS[2]
    pairs = ATTN_HEADS // 2
    return pl.pallas_call(
        _attn_prompt_kernel,
        grid=(n, nj),
        in_specs=[pl.BlockSpec((1, tm, D_MODEL), lambda b, j: (b, j, 0)), pl.BlockSpec((1, D_MODEL), const),
                  pl.BlockSpec((D_MODEL, QKV_COLS), const, pipeline_mode=pl.Buffered(1))],
        out_specs=[pl.BlockSpec((1, ATTN_SUPER, ATTN_WIDTH), lambda b, j: (b, j // SUB_STEPS, 0))]
        + [tail_spec(w) for w in WINDOWS],
        out_shape=[jax.ShapeDtypeStruct((n, s, ATTN_WIDTH), BF16)]
        + [jax.ShapeDtypeStruct((n, w, KV_WIDTH), F32) for w in WINDOWS],
        scratch_shapes=[
            pltpu.VMEM((QKV_COLS // LANES, tm, LANES), F32),
            pltpu.VMEM((tm, ATTN_WIDTH), BF16),
            pltpu.VMEM((ATTN_BLOCK + tm, KV_WIDTH), BF16),
            pltpu.VMEM((d1, tm // d1, ATTN_WIDTH), BF16),
            pltpu.VMEM((d1, ATTN_BLOCK + tm // d1, KV_WIDTH), BF16),
            pltpu.VMEM((d2, ATTN_SUPER // d2, ATTN_WIDTH), BF16),
            pltpu.VMEM((d2, ATTN_BLOCK + ATTN_SUPER // d2, KV_WIDTH), BF16),
            pltpu.VMEM((3, pairs, ATTN_SUPER, LANES), F32),
            pltpu.VMEM((3, pairs, ATTN_SUPER, LANES), F32),
        ],
        compiler_params=pltpu.CompilerParams(
            dimension_semantics=("parallel", "arbitrary"), vmem_limit_bytes=VMEM_LIMIT),
        name="attn_prompt",
    )(x, g1, w_qkv)


def _resident(shape):
    return pl.BlockSpec(shape, lambda *_: (0,) * len(shape), pipeline_mode=pl.Buffered(1))


def _merge_body(x_ref, yp_ref, z_ref, oa_ref, g1_ref, ng_ref, wg_ref, ws_ref, wa_ref, wo_ref, out_ref, ys_scr,
                fillers=()):
    fillers = list(fillers) + [None] * 4

    def fill():
        run = fillers.pop(0)
        if run is not None:
            run()

    xf = x_ref[...]
    h = _rmsnorm_rows(xf, g1_ref[...]).astype(BF16)
    gates = _dot(h, wg_ref[...])
    fill()
    gates = 1.0 / (1.0 + jnp.exp(-gates))
    gw = D_INNER // N_GROUPS
    for g in range(N_GROUPS):
        cols = slice(g * gw, (g + 1) * gw)
        y = yp_ref[:, cols].astype(F32) * _silu(z_ref[:, cols].astype(F32))
        ys_scr[:, cols] = _rmsnorm_rows(y, ng_ref[:, cols]).astype(BF16)
    y_proj = _dot(ys_scr[...], ws_ref[...])
    fill()
    merged = (gates[:, 0:D_MODEL] * y_proj
              + gates[:, D_MODEL:2 * D_MODEL] * _dot(oa_ref[...].astype(BF16), wa_ref[...]))
    fill()
    out_ref[...] = xf + _dot(merged.astype(BF16), wo_ref[...])
    fill()


def _merge_kernel(x_ref, yp_ref, z_ref, oa_ref, g1_ref, ng_ref, wg_ref, ws_ref, wa_ref, wo_ref, out_ref, ys_scr):
    _merge_body(x_ref, yp_ref, z_ref, oa_ref, g1_ref, ng_ref, wg_ref, ws_ref, wa_ref, wo_ref, out_ref, ys_scr)


def _merge_specs(tm):
    row = lambda i: (i, 0)
    return ([pl.BlockSpec((tm, D_MODEL), row), pl.BlockSpec((tm, D_INNER), row),
             pl.BlockSpec((tm, D_INNER), row), pl.BlockSpec((tm, ATTN_WIDTH), row),
             _resident((1, D_MODEL)), _resident((1, D_INNER)), _resident((D_MODEL, 2 * D_MODEL)),
             _resident((D_INNER, D_MODEL)), _resident((ATTN_WIDTH, D_MODEL)), _resident((D_MODEL, D_MODEL))],
            pl.BlockSpec((tm, D_MODEL), row))


def _merge(x2d, y_pre, z, o_attn, g1, ng, w_gates, w_ssm_out, w_attn_out, w_o, *, tm):
    m = x2d.shape[0]
    in_specs, out_spec = _merge_specs(tm)
    return pl.pallas_call(
        _merge_kernel,
        grid=(m // tm,),
        in_specs=in_specs,
        out_specs=out_spec,
        out_shape=jax.ShapeDtypeStruct((m, D_MODEL), F32),
        scratch_shapes=[pltpu.VMEM((tm, D_INNER), BF16)],
        compiler_params=pltpu.CompilerParams(
            dimension_semantics=("parallel",), vmem_limit_bytes=VMEM_LIMIT),
        name="merge",
    )(x2d, y_pre, z, o_attn, g1, ng, w_gates, w_ssm_out, w_attn_out, w_o)


FF_CHUNK = 256


def _ffn_body(x_ref, g2_ref, gf_ref, wg_ref, wu_ref, wd_ref, out_ref, act_scr, fillers=()):
    fillers = list(fillers)
    n_chunks = D_FF // FF_CHUNK
    xf = x_ref[...]
    h = _rmsnorm_rows(xf, g2_ref[...]).astype(BF16)
    for c in range(n_chunks):
        cols = slice(c * FF_CHUNK, (c + 1) * FF_CHUNK)
        gate = _dot(h, wg_ref[:, cols])
        act_scr[:, cols] = (_silu(gate) * _dot(h, wu_ref[:, cols])).astype(BF16)
        for _ in range(-(-len(fillers) // (n_chunks - c))):
            fillers.pop(0)()
    out_ref[...] = _rmsnorm_rows(xf + _dot(act_scr[...], wd_ref[...]), gf_ref[...])


def _ffn_kernel(x_ref, g2_ref, gf_ref, wg_ref, wu_ref, wd_ref, out_ref, act_scr):
    _ffn_body(x_ref, g2_ref, gf_ref, wg_ref, wu_ref, wd_ref, out_ref, act_scr)


def _ffn_specs(tm):
    row = lambda i: (i, 0)
    return ([pl.BlockSpec((tm, D_MODEL), row), _resident((1, D_MODEL)), _resident((1, D_MODEL)),
             _resident((D_MODEL, D_FF)), _resident((D_MODEL, D_FF)), _resident((D_FF, D_MODEL))],
            pl.BlockSpec((tm, D_MODEL), row))


def _ffn(x2d, g2, gf, w_gate, w_up, w_down, *, tm):
    m = x2d.shape[0]
    in_specs, out_spec = _ffn_specs(tm)
    return pl.pallas_call(
        _ffn_kernel,
        grid=(m // tm,),
        in_specs=in_specs,
        out_specs=out_spec,
        out_shape=jax.ShapeDtypeStruct((m, D_MODEL), F32),
        scratch_shapes=[pltpu.VMEM((tm, D_FF), BF16)],
        compiler_params=pltpu.CompilerParams(
            dimension_semantics=("parallel",), vmem_limit_bytes=VMEM_LIMIT),
        name="ffn",
    )(x2d, g2, gf, w_gate, w_up, w_down)


def _ssd_sample_pre_kernel(x_ref, g1_ref, w_ref, cs_ref, cw_ref, cb_ref, dtb_ref, alog_ref, e_ref,
                           z_ref, xs_ref, b_ref, c_ref, xdt_t_ref, da_t_ref, conv_out_ref):
    h = _rmsnorm_rows(x_ref[...], g1_ref[...]).astype(BF16)
    z_ref[...] = _dot(h, w_ref[:, 0:D_INNER])
    xbc = _dot(h, w_ref[:, D_INNER:D_INNER + CONV_DIM])
    dt_raw = _dot(h, w_ref[:, D_INNER + CONV_DIM:W_SSD_COLS])
    acc = cb_ref[...] + cw_ref[CONV_W - 1:CONV_W, :] * xbc
    for k in range(CONV_W - 1):
        hist = cs_ref[k]
        acc = acc + cw_ref[k:k + 1, :] * hist
        if k > 0:
            conv_out_ref[k - 1] = hist
    conv_out_ref[CONV_W - 2] = xbc
    xc = _silu(acc)
    xs = xc[:, 0:D_INNER]
    xs_ref[...] = xs
    b_ref[...] = xc[:, D_INNER:D_INNER + N_GROUPS * D_STATE]
    c_ref[...] = xc[:, D_INNER + N_GROUPS * D_STATE:CONV_DIM]
    dt = _softplus(dt_raw + dtb_ref[...])
    da = jnp.exp(dt * (-jnp.exp(alog_ref[...])))
    xdt = xs * _dot_f32_lhs(dt, e_ref[...])
    da_e = _dot_f32_lhs(da, e_ref[...])
    nb = x_ref.shape[0]
    for k in range(D_INNER // nb):
        xdt_t_ref[k * nb:(k + 1) * nb, :] = xdt[:, k * nb:(k + 1) * nb].T
        da_t_ref[k * nb:(k + 1) * nb, :] = da_e[:, k * nb:(k + 1) * nb].T


def _ssd_sample_pre(x2d, conv_state_t, p):
    nb = x2d.shape[0]
    full = lambda shape: pl.BlockSpec(shape, lambda i: (0,) * len(shape))
    return pl.pallas_call(
        _ssd_sample_pre_kernel,
        grid=(1,),
        in_specs=[full((nb, D_MODEL)), full((1, D_MODEL)), full((D_MODEL, W_SSD_COLS)),
                  full((CONV_W - 1, nb, CONV_DIM)), full((CONV_W, CONV_DIM)), full((1, CONV_DIM)),
                  full((1, DT_PAD)), full((1, DT_PAD)), full((DT_PAD, D_INNER))],
        out_specs=[full((nb, D_INNER)), full((nb, D_INNER)), full((nb, N_GROUPS * D_STATE)),
                   full((nb, N_GROUPS * D_STATE)), full((D_INNER, nb)), full((D_INNER, nb)),
                   full((CONV_W - 1, nb, CONV_DIM))],
        out_shape=[jax.ShapeDtypeStruct((nb, D_INNER), F32), jax.ShapeDtypeStruct((nb, D_INNER), F32),
                   jax.ShapeDtypeStruct((nb, N_GROUPS * D_STATE), F32),
                   jax.ShapeDtypeStruct((nb, N_GROUPS * D_STATE), F32),
                   jax.ShapeDtypeStruct((D_INNER, nb), F32), jax.ShapeDtypeStruct((D_INNER, nb), F32),
                   jax.ShapeDtypeStruct((CONV_W - 1, nb, CONV_DIM), F32)],
        compiler_params=pltpu.CompilerParams(vmem_limit_bytes=VMEM_LIMIT),
        name="ssd_sample_pre",
    )(x2d, p["g1"], p["w_ssd"], conv_state_t, p["conv_w"], p["conv_b"], p["dtb"], p["alog"], p["expand"])


def _ssd_state_group(i, g, st_ref, xdt_ref, da_ref, b_ref, c_ref, xs_ref, dskip_ref, st_out_ref, y_ref):
    grows = D_INNER // N_GROUPS
    rows = slice(g * grows, (g + 1) * grows)
    s_new = st_ref[i, rows, :] * da_ref[i, rows, :] + xdt_ref[i, rows, :] * b_ref[i, g:g + 1, :]
    st_out_ref[i, rows, :] = s_new
    y_t = _dot_nt(c_ref[i].astype(BF16), s_new.astype(BF16))
    y_ref[i, :, rows] = (y_t[g:g + 1, :] + dskip_ref[:, rows] * xs_ref[i, :, rows]).astype(y_ref.dtype)


def _merge_decode_ssd_kernel(x_ref, yp_ref, z_ref, oa_ref, g1_ref, ng_ref, wg_ref, ws_ref, wa_ref, wo_ref,
                             st_ref, xdt_ref, da_ref, b_ref, c_ref, xs_ref, dskip_ref,
                             out_ref, st_out_ref, y_ref, ys_scr, *, bb):
    def group_thunk(g):
        def run():
            for i in range(bb):
                _ssd_state_group(i, g, st_ref, xdt_ref, da_ref, b_ref, c_ref, xs_ref, dskip_ref,
                                 st_out_ref, y_ref)
        return run

    _merge_body(x_ref, yp_ref, z_ref, oa_ref, g1_ref, ng_ref, wg_ref, ws_ref, wa_ref, wo_ref, out_ref, ys_scr,
                [group_thunk(g) for g in range(N_GROUPS)])


def _merge_decode_ssd(x2d, y_pre, z, o_attn, g1, ng, w_gates, w_ssm_out, w_attn_out, w_o,
                      state, xdt_t, da_t, bm, cm, xs, dskip_e, *, tm):
    m = x2d.shape[0]
    nb = state.shape[0]
    steps = m // tm
    assert nb % steps == 0
    bb = nb // steps
    in_specs, out_spec = _merge_specs(tm)
    blk3 = lambda shape: pl.BlockSpec((bb,) + shape, lambda i: (i, 0, 0))
    x1, st_new, y_s = pl.pallas_call(
        functools.partial(_merge_decode_ssd_kernel, bb=bb),
        grid=(steps,),
        in_specs=in_specs + [blk3((D_INNER, D_STATE)), blk3((D_INNER, 1)), blk3((D_INNER, 1)),
                             blk3((N_GROUPS, D_STATE)), blk3((N_GROUPS, D_STATE)), blk3((1, D_INNER)),
                             _resident((1, D_INNER))],
        out_specs=[out_spec, blk3((D_INNER, D_STATE)), blk3((1, D_INNER))],
        out_shape=[jax.ShapeDtypeStruct((m, D_MODEL), F32), jax.ShapeDtypeStruct((nb, D_INNER, D_STATE), F32),
                   jax.ShapeDtypeStruct((nb, 1, D_INNER), BF16)],
        scratch_shapes=[pltpu.VMEM((tm, D_INNER), BF16)],
        compiler_params=pltpu.CompilerParams(
            dimension_semantics=("parallel",), vmem_limit_bytes=VMEM_LIMIT),
        name="merge_decode_ssd",
    )(x2d, y_pre, z, o_attn, g1, ng, w_gates, w_ssm_out, w_attn_out, w_o,
      state, xdt_t.T.reshape(nb, D_INNER, 1), da_t.T.reshape(nb, D_INNER, 1),
      bm.reshape(nb, N_GROUPS, D_STATE), cm.reshape(nb, N_GROUPS, D_STATE), xs.reshape(nb, 1, D_INNER), dskip_e)
    return x1, st_new, y_s.reshape(nb, D_INNER)


def _proj_rows_kernel(x_ref, g1_ref, w_ref, out_ref):
    h = _rmsnorm_rows(x_ref[...], g1_ref[...]).astype(BF16)
    out_ref[...] = _dot(h, w_ref[...])


def _proj_rows(x2d, g1, w):
    nb, cols = x2d.shape[0], w.shape[1]
    full = lambda shape: pl.BlockSpec(shape, lambda i: (0,) * len(shape))
    return pl.pallas_call(
        _proj_rows_kernel,
        grid=(1,),
        in_specs=[full((nb, D_MODEL)), full((1, D_MODEL)), full((D_MODEL, cols))],
        out_specs=full((nb, cols)),
        out_shape=jax.ShapeDtypeStruct((nb, cols), F32),
        compiler_params=pltpu.CompilerParams(vmem_limit_bytes=VMEM_LIMIT),
        name="proj_rows",
    )(x2d, g1, w)


def _attn_sample_head(i, h, qkv_c_ref, cache_refs, o_c_ref):
    kv0 = 3 * ATTN_WIDTH
    hrows = slice(h * ATTN_DIM, (h + 1) * ATTN_DIM)
    scores, new_scores = [], []
    for g, c_ref in enumerate(cache_refs):
        w, dil = WINDOWS[g], DILATIONS[g]
        q_col = qkv_c_ref[i, g * ATTN_WIDTH + h * ATTN_DIM:g * ATTN_WIDTH + (h + 1) * ATTN_DIM, :]
        q_col = q_col * (ATTN_DIM ** -0.5)
        k_lo = kv0 + g * KV_WIDTH + h * ATTN_DIM
        k_col = qkv_c_ref[i, k_lo:k_lo + ATTN_DIM, :]
        s = jnp.sum(c_ref[i, hrows, :] * q_col, axis=0, keepdims=True)
        pos = lax.broadcasted_iota(jnp.int32, (1, w), 1)
        on_stride = (pos & (dil - 1)) == 0
        s = jnp.where(on_stride, s - _alibi_slope(g, h) * (w - pos).astype(F32), NEG_BIG)
        scores.append(s)
        new_scores.append(jnp.sum(q_col * k_col, axis=0, keepdims=True))
    m = new_scores[0]
    for g in range(3):
        m = jnp.maximum(m, jnp.maximum(new_scores[g], jnp.max(scores[g], axis=-1, keepdims=True)))
    den = jnp.zeros((1, 1), F32)
    acc = jnp.zeros((ATTN_DIM, 1), F32)
    for g, c_ref in enumerate(cache_refs):
        p = jnp.exp(scores[g] - m)
        p_new = jnp.exp(new_scores[g] - m)
        den = den + jnp.sum(p, axis=-1, keepdims=True) + p_new
        v_lo = kv0 + g * KV_WIDTH + ATTN_WIDTH + h * ATTN_DIM
        v_col = qkv_c_ref[i, v_lo:v_lo + ATTN_DIM, :]
        v_t = c_ref[i, ATTN_WIDTH + h * ATTN_DIM:ATTN_WIDTH + (h + 1) * ATTN_DIM, :]
        acc = acc + jnp.sum(v_t * p, axis=-1, keepdims=True) + p_new * v_col
    o_c_ref[i, hrows, :] = acc / den


def _ffn_decode_attn_kernel(x_ref, g2_ref, gf_ref, wg_ref, wu_ref, wd_ref, qkv_c_ref, c0_ref, c1_ref, c2_ref,
                            out_ref, o_c_ref, act_scr, *, bb):
    heads = [functools.partial(_attn_sample_head, i, h, qkv_c_ref, (c0_ref, c1_ref, c2_ref), o_c_ref)
             for i in range(bb) for h in range(ATTN_HEADS)]
    _ffn_body(x_ref, g2_ref, gf_ref, wg_ref, wu_ref, wd_ref, out_ref, act_scr, heads)


def _ffn_decode_attn(x2d, g2, gf, w_gate, w_up, w_down, qkv, caches, *, tm):
    m = x2d.shape[0]
    nb = qkv.shape[0]
    steps = m // tm
    assert nb % steps == 0
    bb = nb // steps
    views = []
    for g, c in enumerate(caches):
        assert c.shape[1] == WINDOWS[g]
        views.append(jnp.transpose(c, (0, 2, 3, 4, 1)).reshape(nb, KV_WIDTH, WINDOWS[g]))
    in_specs, out_spec = _ffn_specs(tm)
    y, o_c = pl.pallas_call(
        functools.partial(_ffn_decode_attn_kernel, bb=bb),
        grid=(steps,),
        in_specs=in_specs + [pl.BlockSpec((bb, QKV_COLS, 1), lambda i: (i, 0, 0))]
        + [pl.BlockSpec((bb, KV_WIDTH, w), lambda i: (i, 0, 0)) for w in WINDOWS],
        out_specs=[out_spec, pl.BlockSpec((bb, ATTN_WIDTH, 1), lambda i: (i, 0, 0))],
        out_shape=[jax.ShapeDtypeStruct((m, D_MODEL), F32), jax.ShapeDtypeStruct((nb, ATTN_WIDTH, 1), F32)],
        scratch_shapes=[pltpu.VMEM((tm, D_FF), BF16)],
        compiler_params=pltpu.CompilerParams(
            dimension_semantics=("parallel",), vmem_limit_bytes=VMEM_LIMIT),
        name="ffn_decode_attn",
    )(x2d, g2, gf, w_gate, w_up, w_down, qkv.reshape(nb, QKV_COLS, 1), *views)
    return y, o_c.reshape(nb, ATTN_WIDTH)


def _prep_attn_weight(w_in):
    cols = [w_in[:, OFF_Q:OFF_Q + 3 * ATTN_WIDTH]]
    for g in range(3):
        cols.append(w_in[:, OFF_K + g * ATTN_WIDTH:OFF_K + (g + 1) * ATTN_WIDTH])
        cols.append(w_in[:, OFF_V + g * ATTN_WIDTH:OFF_V + (g + 1) * ATTN_WIDTH])
    return jnp.concatenate(cols, axis=1).astype(BF16)


def _layers(x, x_s, state_ssm, state_conv, caches, wp):
    n, s, _ = x.shape
    nb = x_s.shape[0]
    p = wp["ssd"]
    ffn_w = (wp["g2"], wp["gf"], wp["w_ffn_gate"], wp["w_ffn_up"], wp["w_ffn_down"])
    merge_w = (p["g1"], wp["ng"], wp["w_gates"], wp["w_ssm_out"], wp["w_attn_out"], wp["w_o"])

    y_pre, z, st, conv = _ssd_prompt(x, **p)
    o_attn, t0, t1, t2 = _attn_prompt(x, p["g1"], wp["w_qkv"])
    z_s, xs, bm, cm, xdt_t, da_t, conv_s = _ssd_sample_pre(x_s, jnp.transpose(state_conv, (1, 0, 2)), p)
    qkv_s = _proj_rows(x_s, p["g1"], wp["w_qkv"])

    x1, st_s, y_pre_s = _merge_decode_ssd(
        x.reshape(n * s, D_MODEL), y_pre.reshape(n * s, D_INNER), z.reshape(n * s, D_INNER),
        o_attn.reshape(n * s, ATTN_WIDTH), *merge_w,
        state_ssm.reshape(nb, D_INNER, D_STATE), xdt_t, da_t, bm, cm, xs, p["dskip_e"], tm=512)

    y, o_s = _ffn_decode_attn(x1, *ffn_w, qkv_s, caches, tm=512)

    x1_s = _merge(x_s, y_pre_s, z_s, o_s, *merge_w, tm=nb)
    y_s = _ffn(x1_s, *ffn_w, tm=nb)
    prompt = (y.reshape(n, s, D_MODEL), st, conv, (t0, t1, t2))
    sample = (y_s, st_s, conv_s, qkv_s[:, 3 * ATTN_WIDTH:])
    return prompt, sample


def _prep_weights(norm1_g, w_in, conv_w, conv_b, dt_bias, a_log, d_skip, ssm_norm_g, w_ssm_out, w_attn_out,
                  w_o, norm2_g, w_ffn_gate, w_ffn_up, w_ffn_down, norm_f_g):
    return dict(
        ssd=_prep_ssd_params(norm1_g, w_in, conv_w, conv_b, dt_bias, a_log, d_skip),
        w_qkv=_prep_attn_weight(w_in), ng=ssm_norm_g.reshape(1, D_INNER),
        w_gates=w_in[:, OFF_GS:OFF_GS + 2 * D_MODEL].astype(BF16),
        w_ssm_out=w_ssm_out.astype(BF16), w_attn_out=w_attn_out.astype(BF16), w_o=w_o.astype(BF16),
        g2=norm2_g.reshape(1, D_MODEL), gf=norm_f_g.reshape(1, D_MODEL),
        w_ffn_gate=w_ffn_gate.astype(BF16), w_ffn_up=w_ffn_up.astype(BF16),
        w_ffn_down=w_ffn_down.astype(BF16))


def kernel(x_prompt, x_sample, state_ssm, state_conv, cache_kv_w128, cache_kv_w512, cache_kv_w2048, norm1_g, w_in, conv_w, conv_b, dt_bias, a_log, d_skip, ssm_norm_g, w_ssm_out, w_attn_out, w_o, norm2_g, w_ffn_gate, w_ffn_up, w_ffn_down, norm_f_g):
    wp = _prep_weights(norm1_g[0], w_in[0], conv_w[0], conv_b[0], dt_bias[0], a_log[0], d_skip[0],
                       ssm_norm_g[0], w_ssm_out[0], w_attn_out[0], w_o[0], norm2_g[0], w_ffn_gate[0],
                       w_ffn_up[0], w_ffn_down[0], norm_f_g)
    n, s, _ = x_prompt.shape
    nb = x_sample.shape[0]
    (y_p, st_p, conv_p, tails), (y_s, st_s, conv_s, kv_s) = _layers(
        x_prompt, x_sample.reshape(nb, D_MODEL), state_ssm[0], state_conv[0],
        (cache_kv_w128[0], cache_kv_w512[0], cache_kv_w2048[0]), wp)
    kv_tail_shape = (2, ATTN_HEADS, ATTN_DIM)
    outs = [y_p, y_s.reshape(nb, 1, D_MODEL),
            st_p.reshape(1, n, N_HEADS, HEAD_DIM, D_STATE), st_s.reshape(1, nb, N_HEADS, HEAD_DIM, D_STATE),
            conv_p.reshape(1, n, CONV_W - 1, CONV_DIM), jnp.transpose(conv_s, (1, 0, 2)).reshape(1, nb, CONV_W - 1, CONV_DIM)]
    for g in range(3):
        outs.append(tails[g].reshape((1, n, WINDOWS[g]) + kv_tail_shape))
        outs.append(kv_s[:, g * KV_WIDTH:(g + 1) * KV_WIDTH].reshape((1, nb, 1) + kv_tail_shape))
    return tuple(outs)
```

```python
import functools

import jax
import jax.numpy as jnp
from jax import lax
from jax.experimental import pallas as pl
from jax.experimental.pallas import tpu as pltpu

F32 = jnp.float32
BF16 = jnp.bfloat16

D_MODEL = 1024
D_INNER = 2048
HEAD_DIM = 64
N_HEADS = 32
N_GROUPS = 4
HEADS_PER_SSM_GROUP = N_HEADS // N_GROUPS
D_STATE = 128
CONV_W = 4
CONV_DIM = D_INNER + 2 * N_GROUPS * D_STATE
CHUNK = 128
DT_PAD = 128
ATTN_HEADS = 4
ATTN_DIM = 64
ATTN_WIDTH = ATTN_HEADS * ATTN_DIM
WINDOWS = (128, 512, 2048)
DILATIONS = (1, 4, 16)
ATTN_BLOCK = 128
D_FF = 2816
LANES = 128
RMS_EPS = 1e-6
NEG_BIG = -1e30
OFF_XBC = D_INNER
OFF_DT = OFF_XBC + CONV_DIM
OFF_Q = OFF_DT + N_HEADS
OFF_K = OFF_Q + 3 * ATTN_WIDTH
OFF_V = OFF_K + 3 * ATTN_WIDTH
OFF_GS = OFF_V + 3 * ATTN_WIDTH
W_SSD_COLS = D_INNER + CONV_DIM + DT_PAD
VMEM_LIMIT = 52 * 1024 * 1024


def _dot(a, b):
    return jnp.dot(a, b, preferred_element_type=F32)


def _dot_nt(a, b):
    return lax.dot_general(a, b, (((1,), (1,)), ((), ())), preferred_element_type=F32)


def _split3(v):
    hi = v.astype(BF16)
    r1 = v - hi.astype(F32)
    mid = r1.astype(BF16)
    lo = (r1 - mid.astype(F32)).astype(BF16)
    return hi, mid, lo


def _dot_f32_rhs(a_bf16_exact, v):
    hi, mid, lo = _split3(v)
    return _dot(a_bf16_exact, hi) + _dot(a_bf16_exact, mid) + _dot(a_bf16_exact, lo)


def _dot_f32_lhs(v, b_bf16_exact):
    hi, mid, lo = _split3(v)
    return _dot(hi, b_bf16_exact) + _dot(mid, b_bf16_exact) + _dot(lo, b_bf16_exact)


def _to_cols(rows):
    r = rows.shape[0]
    padded = jnp.concatenate([rows, jnp.zeros((LANES - r, LANES), F32)], axis=0) if r < LANES else rows
    eye = (lax.broadcasted_iota(jnp.int32, (LANES, LANES), 0)
           == lax.broadcasted_iota(jnp.int32, (LANES, LANES), 1)).astype(BF16)
    hi, mid, lo = _split3(padded)
    return _dot_nt(eye, hi) + _dot_nt(eye, mid) + _dot_nt(eye, lo)


def _rmsnorm_rows(xf, g_row):
    ms = jnp.mean(xf * xf, axis=-1, keepdims=True)
    return xf * lax.rsqrt(ms + RMS_EPS) * g_row


def _silu(v):
    h = 0.5 * v
    return h + h * jnp.tanh(h)


def _log1p(e):
    u = 1.0 + e
    return jnp.where(u == 1.0, e, jnp.log(u) * (e / (u - 1.0)))


def _softplus(v):
    return jnp.maximum(v, 0.0) + _log1p(jnp.exp(-jnp.abs(v)))


def _ssd_chunk(out_rows, xc_ref, xslot, dt_ref, dslot, st_ref, y_ref, e_ref, dtb_row, a_row, dskip_row,
               fillers):
    tri_r = lax.broadcasted_iota(jnp.int32, (CHUNK, CHUNK), 0)
    tri_c = lax.broadcasted_iota(jnp.int32, (CHUNK, CHUNK), 1)
    causal = tri_r >= tri_c
    ltri = causal.astype(BF16)
    lane_lo = tri_c < HEAD_DIM
    fillers = list(fillers)
    n_slots = N_HEADS // 2

    def run_fillers(slots_left):
        for _ in range(-(-len(fillers) // slots_left)):
            fillers.pop(0)()

    dt = _softplus(dt_ref[dslot] + dtb_row)
    adt = dt * a_row
    acum = _dot_f32_rhs(ltri, adt)
    acum_dt_t = (acum - jnp.log(dt)).T
    a_last = acum[CHUNK - 1:CHUNK, :]
    w_t = (dt * jnp.exp(a_last - acum)).T
    cdec = jnp.exp(a_last)
    cdec_e = _dot_f32_lhs(jnp.broadcast_to(cdec, (8, DT_PAD)), e_ref[...])[0:1, :]

    for g in range(N_GROUPS):
        b_g = xc_ref[xslot, :, D_INNER + g * D_STATE:D_INNER + (g + 1) * D_STATE]
        c_g = xc_ref[xslot, :, D_INNER + N_GROUPS * D_STATE + g * D_STATE:
                     D_INNER + N_GROUPS * D_STATE + (g + 1) * D_STATE]
        b_bf = b_g.astype(BF16)
        c_bf = c_g.astype(BF16)
        cb = _dot_nt(c_bf, b_bf)
        b_t = b_g.T
        gcols = slice(g * HEADS_PER_SSM_GROUP * HEAD_DIM, (g + 1) * HEADS_PER_SSM_GROUP * HEAD_DIM)
        y_off = _dot(c_bf, st_ref[:, gcols].astype(BF16))
        for pr in range(HEADS_PER_SSM_GROUP // 2):
            h0 = g * HEADS_PER_SSM_GROUP + 2 * pr
            pcols = slice(h0 * HEAD_DIM, (h0 + 2) * HEAD_DIM)
            xs_f32 = xc_ref[xslot, :, pcols]
            xs_pair = xs_f32.astype(BF16)
            xs_diag = jnp.concatenate([jnp.where(lane_lo, xs_pair, jnp.zeros_like(xs_pair)),
                                       jnp.where(lane_lo, jnp.zeros_like(xs_pair), xs_pair)], axis=0)
            a_b, wmats, smats = [], [], []
            for k in range(2):
                h = h0 + k
                ab = jnp.broadcast_to(acum[:, h:h + 1], (CHUNK, CHUNK))
                a_b.append(ab)
                dec = jnp.exp(jnp.where(causal, ab - acum_dt_t[h:h + 1, :], NEG_BIG))
                wmats.append((cb * dec).astype(BF16))
                smats.append((b_t * w_t[h:h + 1, :]).astype(BF16))
            y_pair = _dot(jnp.concatenate(wmats, axis=1), xs_diag)
            s_pair = _dot(jnp.concatenate(smats, axis=1), xs_diag)
            ea_pair = jnp.exp(jnp.where(lane_lo, a_b[0], a_b[1]))
            lo = (2 * pr) * HEAD_DIM
            y_pre = y_pair + y_off[:, lo:lo + 2 * HEAD_DIM] * ea_pair + dskip_row[:, pcols] * xs_f32
            y_ref[0, out_rows, pcols] = y_pre.astype(y_ref.dtype)
            st_ref[:, pcols] = st_ref[:, pcols] * cdec_e[:, pcols] + s_pair
            run_fillers(n_slots - (g * (HEADS_PER_SSM_GROUP // 2) + pr))
    assert not fillers


def _ssd_prompt_kernel(x_ref, g1_ref, w_ref, cw_ref, cb_ref, dtb_ref, alog_ref, dskip_ref,
                       e_ref, y_ref, z_out_ref, st_out_ref, conv_out_ref,
                       ext_ref, xc_ref, dt_ref, st_ref, h_scr, *, tl):
    j = pl.program_id(1)
    nj = pl.num_programs(1)
    nchunk = tl // CHUNK
    hist = 8

    @pl.when(j == 0)
    def _():
        st_ref[...] = jnp.zeros_like(st_ref)
        ext_ref[1, CHUNK:CHUNK + hist, :] = jnp.zeros((hist, CONV_DIM), F32)

    dtb_row = dtb_ref[...]
    a_row = -jnp.exp(alog_ref[...])
    dskip_row = dskip_ref[...]
    piece = 512
    pieces = ([("z", lo, lo + piece) for lo in range(0, D_INNER, piece)]
              + [("xbc", lo, lo + piece) for lo in range(0, CONV_DIM, piece)] + [("dt", 0, DT_PAD)])
    cblk = 256

    def rows_of(k):
        return pl.ds(pl.multiple_of(k * CHUNK, CHUNK), CHUNK)

    def project_thunks(k, slot):
        rows = rows_of(k)

        def normalize():
            h_scr[...] = _rmsnorm_rows(x_ref[0, rows, :], g1_ref[...]).astype(BF16)
            ext_ref[slot, 0:hist, :] = ext_ref[1 - slot, CHUNK:CHUNK + hist, :]

        def piece_thunk(kind, lo, hi):
            def run():
                if kind == "z":
                    z_out_ref[0, rows, lo:hi] = _dot(h_scr[...], w_ref[:, lo:hi]).astype(z_out_ref.dtype)
                elif kind == "xbc":
                    ext_ref[slot, hist:hist + CHUNK, lo:hi] = _dot(
                        h_scr[...], w_ref[:, D_INNER + lo:D_INNER + hi])
                else:
                    dt_ref[slot] = _dot(h_scr[...], w_ref[:, D_INNER + CONV_DIM:W_SSD_COLS])
            return run

        return [normalize] + [piece_thunk(*p) for p in pieces]

    def conv_thunks(slot):
        def block_thunk(cbk):
            def run():
                cols = slice(cbk * cblk, (cbk + 1) * cblk)
                win = ext_ref[slot, :, cols]
                acc = cb_ref[:, cols] + cw_ref[CONV_W - 1:CONV_W, cols] * win[hist:, :]
                for t in range(1, CONV_W):
                    tap = cw_ref[CONV_W - 1 - t:CONV_W - t, cols]
                    acc = acc + tap * pltpu.roll(win, t, axis=0)[hist:, :]
                xc_ref[slot, :, cols] = _silu(acc)
            return run

        return [block_thunk(c) for c in range(CONV_DIM // cblk)]

    def interleave(a, b):
        out = []
        for i in range(max(len(a), len(b))):
            out += a[i:i + 1] + b[i:i + 1]
        return out

    def scan(k, slot, fillers):
        _ssd_chunk(rows_of(k), xc_ref, slot, dt_ref, slot, st_ref, y_ref, e_ref,
                   dtb_row, a_row, dskip_row, fillers)

    for thunk in project_thunks(0, 0) + interleave(conv_thunks(0), project_thunks(1, 1)):
        thunk()

    def body(m, carry):
        for e in range(2):
            k = 2 * m + e
            scan(k, e, interleave(conv_thunks(1 - e), project_thunks(k + 2, e)))
        return carry

    lax.fori_loop(0, (nchunk - 2) // 2, body, 0)
    scan(nchunk - 2, 0, conv_thunks(1))
    scan(nchunk - 1, 1, [])

    @pl.when(j == nj - 1)
    def _():
        last = (nchunk - 1) % 2
        conv_out_ref[0] = ext_ref[last, hist + CHUNK - (CONV_W - 1):hist + CHUNK, :]
        for k in range(D_INNER // D_STATE):
            st_out_ref[0, k * D_STATE:(k + 1) * D_STATE, :] = st_ref[:, k * D_STATE:(k + 1) * D_STATE].T


def _ssd_prompt(x, g1, w_ssd, conv_w, conv_b, dtb, alog, dskip_e, expand, *, tl=1024):
    n, s, _ = x.shape
    assert (tl // CHUNK) % 2 == 0 and s % tl == 0
    kern = functools.partial(_ssd_prompt_kernel, tl=tl)
    const = lambda b, j: (0, 0)
    return pl.pallas_call(
        kern,
        grid=(n, s // tl),
        in_specs=[
            pl.BlockSpec((1, tl, D_MODEL), lambda b, j: (b, j, 0)),
            pl.BlockSpec((1, D_MODEL), const),
            pl.BlockSpec((D_MODEL, W_SSD_COLS), const, pipeline_mode=pl.Buffered(1)),
            pl.BlockSpec((CONV_W, CONV_DIM), const),
            pl.BlockSpec((1, CONV_DIM), const),
            pl.BlockSpec((1, DT_PAD), const),
            pl.BlockSpec((1, DT_PAD), const),
            pl.BlockSpec((1, D_INNER), const),
            pl.BlockSpec((DT_PAD, D_INNER), const),
        ],
        out_specs=[
            pl.BlockSpec((1, tl, D_INNER), lambda b, j: (b, j, 0)),
            pl.BlockSpec((1, tl, D_INNER), lambda b, j: (b, j, 0)),
            pl.BlockSpec((1, D_INNER, D_STATE), lambda b, j: (b, 0, 0)),
            pl.BlockSpec((1, CONV_W - 1, CONV_DIM), lambda b, j: (b, 0, 0)),
        ],
        out_shape=[
            jax.ShapeDtypeStruct((n, s, D_INNER), BF16),
            jax.ShapeDtypeStruct((n, s, D_INNER), BF16),
            jax.ShapeDtypeStruct((n, D_INNER, D_STATE), F32),
            jax.ShapeDtypeStruct((n, CONV_W - 1, CONV_DIM), F32),
        ],
        scratch_shapes=[
            pltpu.VMEM((2, CHUNK + 8, CONV_DIM), F32),
            pltpu.VMEM((2, CHUNK, CONV_DIM), F32),
            pltpu.VMEM((2, CHUNK, DT_PAD), F32),
            pltpu.VMEM((D_STATE, D_INNER), F32),
            pltpu.VMEM((CHUNK, D_MODEL), BF16),
        ],
        compiler_params=pltpu.CompilerParams(
            dimension_semantics=("parallel", "arbitrary"), vmem_limit_bytes=VMEM_LIMIT),
        name="ssd_prompt",
    )(x, g1, w_ssd, conv_w, conv_b, dtb, alog, dskip_e, expand)


def _prep_ssd_params(norm1_g, w_in, conv_w, conv_b, dt_bias, a_log, d_skip):
    w_ssd = jnp.concatenate(
        [w_in[:, 0:OFF_DT], w_in[:, OFF_DT:OFF_Q], jnp.zeros((D_MODEL, DT_PAD - N_HEADS), w_in.dtype)],
        axis=1).astype(BF16)
    pad = lambda v: jnp.pad(v, (0, DT_PAD - N_HEADS)).reshape(1, DT_PAD)
    expand = (jnp.arange(DT_PAD)[:, None] == (jnp.arange(D_INNER) // HEAD_DIM)[None, :]).astype(BF16)
    return dict(
        g1=norm1_g.reshape(1, D_MODEL), w_ssd=w_ssd, conv_w=conv_w, conv_b=conv_b.reshape(1, CONV_DIM),
        dtb=pad(dt_bias), alog=pad(a_log), dskip_e=jnp.repeat(d_skip, HEAD_DIM).reshape(1, D_INNER),
        expand=expand)


QKV_COLS = 9 * ATTN_WIDTH
KV_WIDTH = 2 * ATTN_WIDTH
ATTN_TILE = 512
ATTN_SUPER = WINDOWS[2]
SUB_STEPS = ATTN_SUPER // ATTN_TILE


def _alibi_slope(group, head):
    return 2.0 ** (-8.0 * (group * ATTN_HEADS + head + 1) / (3 * ATTN_HEADS))


def _banded_block(group, q, kv, ok):
    dil = DILATIONS[group]
    blk = ATTN_BLOCK
    delta = (lax.broadcasted_iota(jnp.int32, (blk, 2 * blk), 0) + blk
             - lax.broadcasted_iota(jnp.int32, (blk, 2 * blk), 1))
    dist = (dil * delta).astype(F32)
    lane_lo_q = lax.broadcasted_iota(jnp.int32, (blk, 2 * ATTN_DIM), 1) < ATTN_DIM
    lane_lo_v = lax.broadcasted_iota(jnp.int32, (2 * blk, 2 * ATTN_DIM), 1) < ATTN_DIM
    out = []
    for pr in range(ATTN_HEADS // 2):
        cols = slice(pr * 2 * ATTN_DIM, (pr + 1) * 2 * ATTN_DIM)
        vcols = slice(ATTN_WIDTH + pr * 2 * ATTN_DIM, ATTN_WIDTH + (pr + 1) * 2 * ATTN_DIM)
        q_pair, k_pair, v_pair = q[:, cols], kv[:, cols], kv[:, vcols]
        o_pair = None
        lse = []
        for k in range(2):
            sel_q = lane_lo_q if k == 0 else ~lane_lo_q
            sel_v = lane_lo_v if k == 0 else ~lane_lo_v
            q_h = jnp.where(sel_q, q_pair, jnp.zeros_like(q_pair))
            v_h = jnp.where(sel_v, v_pair, jnp.zeros_like(v_pair))
            s = _dot_nt(q_h, k_pair)
            s = jnp.where(ok, s - _alibi_slope(group, 2 * pr + k) * dist, NEG_BIG)
            m = jnp.max(s, axis=-1, keepdims=True)
            p = jnp.exp(s - m)
            den = jnp.sum(p, axis=-1, keepdims=True)
            o_h = _dot(p.astype(BF16), v_h) / den
            o_pair = o_h if o_pair is None else o_pair + o_h
            lse.append(m + jnp.log(den))
        out.append((o_pair, jnp.where(lane_lo_q, lse[0], lse[1])))
    return out


def _attn_prompt_kernel(x_ref, g1_ref, w_ref, o_ref, t0_ref, t1_ref, t2_ref,
                        nat, q0b, kv0b, q1b, kv1b, q2b, kv2b, o_nat, l_nat):
    j = pl.program_id(1)
    nj = pl.num_programs(1)
    p = j % SUB_STEPS
    blk = ATTN_BLOCK
    tm = ATTN_TILE
    qpan = ATTN_WIDTH // LANES
    kvpan = KV_WIDTH // LANES

    @pl.when(j == 0)
    def _():
        kv0b[0:blk, :] = jnp.zeros((blk, KV_WIDTH), BF16)
        kv1b[:, 0:blk, :] = jnp.zeros((DILATIONS[1], blk, KV_WIDTH), BF16)
        kv2b[:, 0:blk, :] = jnp.zeros((DILATIONS[2], blk, KV_WIDTH), BF16)

    @pl.when(j > 0)
    def _():
        kv0b[0:blk, :] = kv0b[tm:tm + blk, :]
        kv1b[:, 0:blk, :] = kv1b[:, blk:2 * blk, :]

    @pl.when((j > 0) & (p == 0))
    def _():
        kv2b[:, 0:blk, :] = kv2b[:, blk:2 * blk, :]

    h = _rmsnorm_rows(x_ref[0], g1_ref[...]).astype(BF16)
    for c in range(0, QKV_COLS // LANES, 2):
        res = _dot(h, w_ref[:, c * LANES:(c + 2) * LANES])
        if c < 3 * qpan:
            res = res * (ATTN_DIM ** -0.5)
        nat[c] = res[:, 0:LANES]
        nat[c + 1] = res[:, LANES:2 * LANES]

    for g, t_ref in enumerate((t0_ref, t1_ref, t2_ref)):
        keep = WINDOWS[g]
        kv_first = 3 * qpan + g * kvpan
        if keep >= tm:
            @pl.when(j >= nj - keep // tm)
            def _():
                for c in range(kvpan):
                    t_ref[0, :, c * LANES:(c + 1) * LANES] = nat[kv_first + c]
        else:
            @pl.when(j == nj - 1)
            def _():
                for c in range(kvpan):
                    t_ref[0, :, c * LANES:(c + 1) * LANES] = nat[kv_first + c, tm - keep:tm, :]

    for c in range(qpan):
        q0b[:, c * LANES:(c + 1) * LANES] = nat[c].astype(BF16)
    for c in range(kvpan):
        kv0b[blk:blk + tm, c * LANES:(c + 1) * LANES] = nat[3 * qpan + c].astype(BF16)
    d1 = DILATIONS[1]
    for r in range(d1):
        for c in range(qpan):
            q1b[r, :, c * LANES:(c + 1) * LANES] = nat[qpan + c, pl.ds(r, tm // d1, stride=d1), :].astype(BF16)
        for c in range(kvpan):
            kv1b[r, blk:2 * blk, c * LANES:(c + 1) * LANES] = (
                nat[3 * qpan + kvpan + c, pl.ds(r, tm // d1, stride=d1), :].astype(BF16))
    d2 = DILATIONS[2]
    sub = tm // d2
    row2 = pl.multiple_of(p * sub, sub)
    for r in range(d2):
        for c in range(qpan):
            q2b[r, pl.ds(row2, sub), c * LANES:(c + 1) * LANES] = (
                nat[2 * qpan + c, pl.ds(r, sub, stride=d2), :].astype(BF16))
        for c in range(kvpan):
            kv2b[r, pl.ds(blk + row2, sub), c * LANES:(c + 1) * LANES] = (
                nat[3 * qpan + 2 * kvpan + c, pl.ds(r, sub, stride=d2), :].astype(BF16))

    key_idx = lax.broadcasted_iota(jnp.int32, (blk, 2 * blk), 1)
    delta = lax.broadcasted_iota(jnp.int32, (blk, 2 * blk), 0) + blk - key_idx
    in_band = (delta >= 0) & (delta <= blk)
    own_key = key_idx >= blk
    base = pl.multiple_of(p * tm, tm)

    def store(group, rows, res):
        for pr, (o_pair, l_pair) in enumerate(res):
            o_nat[group, pr, rows, :] = o_pair
            l_nat[group, pr, rows, :] = l_pair

    for i in range(tm // blk):
        ok = in_band & (own_key | (j > 0)) if i == 0 else in_band
        res = _banded_block(0, q0b[i * blk:(i + 1) * blk, :], kv0b[i * blk:(i + 2) * blk, :], ok)
        store(0, pl.ds(base + i * blk, blk), res)

    ok1 = in_band & (own_key | (j > 0))
    for r in range(d1):
        store(1, pl.ds(base + r, blk, stride=d1), _banded_block(1, q1b[r], kv1b[r], ok1))

    @pl.when(p == SUB_STEPS - 1)
    def _():
        ok2 = in_band & (own_key | (j >= SUB_STEPS))
        unroll = 4

        def g2_body(it, carry):
            for k in range(unroll):
                r = it * unroll + k
                store(2, pl.ds(r, blk, stride=d2), _banded_block(2, q2b[r], kv2b[r], ok2))
            return carry

        lax.fori_loop(0, d2 // unroll, g2_body, 0)

        rb = 256
        for k in range(ATTN_SUPER // rb):
            rows = slice(k * rb, (k + 1) * rb)
            for pr in range(ATTN_HEADS // 2):
                l0, l1, l2 = l_nat[0, pr, rows, :], l_nat[1, pr, rows, :], l_nat[2, pr, rows, :]
                m = jnp.maximum(jnp.maximum(l0, l1), l2)
                e0, e1, e2 = jnp.exp(l0 - m), jnp.exp(l1 - m), jnp.exp(l2 - m)
                o = (e0 * o_nat[0, pr, rows, :] + e1 * o_nat[1, pr, rows, :]
                     + e2 * o_nat[2, pr, rows, :]) / (e0 + e1 + e2)
                o_ref[0, rows, pr * LANES:(pr + 1) * LANES] = o.astype(o_ref.dtype)


def _attn_prompt(x, g1, w_qkv):
    n, s, _ = x.shape
    tm = ATTN_TILE
    nj = s // tm
    assert s % ATTN_SUPER == 0
    const = lambda b, j: (0, 0)

    def tail_spec(keep):
        if keep >= tm:
            first = nj - keep // tm
            return pl.BlockSpec((1, tm, KV_WIDTH), lambda b, j: (b, jnp.maximum(j - first, 0), 0))
        return pl.BlockSpec((1, keep, KV_WIDTH), lambda b, j: (b, 0, 0))

    d1, d2 = DILATIONS[1], DILATIONS[2]
    pairs = ATTN_HEADS // 2
    return pl.pallas_call(
        _attn_prompt_kernel,
        grid=(n, nj),
        in_specs=[pl.BlockSpec((1, tm, D_MODEL), lambda b, j: (b, j, 0)), pl.BlockSpec((1, D_MODEL), const),
                  pl.BlockSpec((D_MODEL, QKV_COLS), const, pipeline_mode=pl.Buffered(1))],
        out_specs=[pl.BlockSpec((1, ATTN_SUPER, ATTN_WIDTH), lambda b, j: (b, j // SUB_STEPS, 0))]
        + [tail_spec(w) for w in WINDOWS],
        out_shape=[jax.ShapeDtypeStruct((n, s, ATTN_WIDTH), BF16)]
        + [jax.ShapeDtypeStruct((n, w, KV_WIDTH), F32) for w in WINDOWS],
        scratch_shapes=[
            pltpu.VMEM((QKV_COLS // LANES, tm, LANES), F32),
            pltpu.VMEM((tm, ATTN_WIDTH), BF16),
            pltpu.VMEM((ATTN_BLOCK + tm, KV_WIDTH), BF16),
            pltpu.VMEM((d1, tm // d1, ATTN_WIDTH), BF16),
            pltpu.VMEM((d1, ATTN_BLOCK + tm // d1, KV_WIDTH), BF16),
            pltpu.VMEM((d2, ATTN_SUPER // d2, ATTN_WIDTH), BF16),
            pltpu.VMEM((d2, ATTN_BLOCK + ATTN_SUPER // d2, KV_WIDTH), BF16),
            pltpu.VMEM((3, pairs, ATTN_SUPER, LANES), F32),
            pltpu.VMEM((3, pairs, ATTN_SUPER, LANES), F32),
        ],
        compiler_params=pltpu.CompilerParams(
            dimension_semantics=("parallel", "arbitrary"), vmem_limit_bytes=VMEM_LIMIT),
        name="attn_prompt",
    )(x, g1, w_qkv)


def _resident(shape):
    return pl.BlockSpec(shape, lambda *_: (0,) * len(shape), pipeline_mode=pl.Buffered(1))


def _merge_body(x_ref, yp_ref, z_ref, oa_ref, g1_ref, ng_ref, wg_ref, ws_ref, wa_ref, wo_ref, out_ref, ys_scr,
                fillers=()):
    fillers = list(fillers) + [None] * 4

    def fill():
        run = fillers.pop(0)
        if run is not None:
            run()

    xf = x_ref[...]
    h = _rmsnorm_rows(xf, g1_ref[...]).astype(BF16)
    gates = _dot(h, wg_ref[...])
    fill()
    gates = 1.0 / (1.0 + jnp.exp(-gates))
    gw = D_INNER // N_GROUPS
    for g in range(N_GROUPS):
        cols = slice(g * gw, (g + 1) * gw)
        y = yp_ref[:, cols].astype(F32) * _silu(z_ref[:, cols].astype(F32))
        ys_scr[:, cols] = _rmsnorm_rows(y, ng_ref[:, cols]).astype(BF16)
    y_proj = _dot(ys_scr[...], ws_ref[...])
    fill()
    merged = (gates[:, 0:D_MODEL] * y_proj
              + gates[:, D_MODEL:2 * D_MODEL] * _dot(oa_ref[...].astype(BF16), wa_ref[...]))
    fill()
    out_ref[...] = xf + _dot(merged.astype(BF16), wo_ref[...])
    fill()


def _merge_kernel(x_ref, yp_ref, z_ref, oa_ref, g1_ref, ng_ref, wg_ref, ws_ref, wa_ref, wo_ref, out_ref, ys_scr):
    _merge_body(x_ref, yp_ref, z_ref, oa_ref, g1_ref, ng_ref, wg_ref, ws_ref, wa_ref, wo_ref, out_ref, ys_scr)


def _merge_specs(tm):
    row = lambda i: (i, 0)
    return ([pl.BlockSpec((tm, D_MODEL), row), pl.BlockSpec((tm, D_INNER), row),
             pl.BlockSpec((tm, D_INNER), row), pl.BlockSpec((tm, ATTN_WIDTH), row),
             _resident((1, D_MODEL)), _resident((1, D_INNER)), _resident((D_MODEL, 2 * D_MODEL)),
             _resident((D_INNER, D_MODEL)), _resident((ATTN_WIDTH, D_MODEL)), _resident((D_MODEL, D_MODEL))],
            pl.BlockSpec((tm, D_MODEL), row))


def _merge(x2d, y_pre, z, o_attn, g1, ng, w_gates, w_ssm_out, w_attn_out, w_o, *, tm):
    m = x2d.shape[0]
    in_specs, out_spec = _merge_specs(tm)
    return pl.pallas_call(
        _merge_kernel,
        grid=(m // tm,),
        in_specs=in_specs,
        out_specs=out_spec,
        out_shape=jax.ShapeDtypeStruct((m, D_MODEL), F32),
        scratch_shapes=[pltpu.VMEM((tm, D_INNER), BF16)],
        compiler_params=pltpu.CompilerParams(
            dimension_semantics=("parallel",), vmem_limit_bytes=VMEM_LIMIT),
        name="merge",
    )(x2d, y_pre, z, o_attn, g1, ng, w_gates, w_ssm_out, w_attn_out, w_o)


FF_CHUNK = 256


def _ffn_body(x_ref, g2_ref, gf_ref, wg_ref, wu_ref, wd_ref, out_ref, act_scr, fillers=()):
    fillers = list(fillers)
    n_chunks = D_FF // FF_CHUNK
    xf = x_ref[...]
    h = _rmsnorm_rows(xf, g2_ref[...]).astype(BF16)
    for c in range(n_chunks):
        cols = slice(c * FF_CHUNK, (c + 1) * FF_CHUNK)
        gate = _dot(h, wg_ref[:, cols])
        act_scr[:, cols] = (_silu(gate) * _dot(h, wu_ref[:, cols])).astype(BF16)
        for _ in range(-(-len(fillers) // (n_chunks - c))):
            fillers.pop(0)()
    out_ref[...] = _rmsnorm_rows(xf + _dot(act_scr[...], wd_ref[...]), gf_ref[...])


def _ffn_kernel(x_ref, g2_ref, gf_ref, wg_ref, wu_ref, wd_ref, out_ref, act_scr):
    _ffn_body(x_ref, g2_ref, gf_ref, wg_ref, wu_ref, wd_ref, out_ref, act_scr)


def _ffn_specs(tm):
    row = lambda i: (i, 0)
    return ([pl.BlockSpec((tm, D_MODEL), row), _resident((1, D_MODEL)), _resident((1, D_MODEL)),
             _resident((D_MODEL, D_FF)), _resident((D_MODEL, D_FF)), _resident((D_FF, D_MODEL))],
            pl.BlockSpec((tm, D_MODEL), row))


def _ffn(x2d, g2, gf, w_gate, w_up, w_down, *, tm):
    m = x2d.shape[0]
    in_specs, out_spec = _ffn_specs(tm)
    return pl.pallas_call(
        _ffn_kernel,
        grid=(m // tm,),
        in_specs=in_specs,
        out_specs=out_spec,
        out_shape=jax.ShapeDtypeStruct((m, D_MODEL), F32),
        scratch_shapes=[pltpu.VMEM((tm, D_FF), BF16)],
        compiler_params=pltpu.CompilerParams(
            dimension_semantics=("parallel",), vmem_limit_bytes=VMEM_LIMIT),
        name="ffn",
    )(x2d, g2, gf, w_gate, w_up, w_down)


def _ssd_sample_pre_kernel(x_ref, g1_ref, w_ref, cs_ref, cw_ref, cb_ref, dtb_ref, alog_ref, e_ref,
                           z_ref, xs_ref, b_ref, c_ref, xdt_ref, da_ref, conv_out_ref):
    h = _rmsnorm_rows(x_ref[...], g1_ref[...]).astype(BF16)
    z_ref[...] = _dot(h, w_ref[:, 0:D_INNER])
    xbc = _dot(h, w_ref[:, D_INNER:D_INNER + CONV_DIM])
    dt_raw = _dot(h, w_ref[:, D_INNER + CONV_DIM:W_SSD_COLS])
    acc = cb_ref[...] + cw_ref[CONV_W - 1:CONV_W, :] * xbc
    for k in range(CONV_W - 1):
        hist = cs_ref[k]
        acc = acc + cw_ref[k:k + 1, :] * hist
        if k > 0:
            conv_out_ref[k - 1] = hist
    conv_out_ref[CONV_W - 2] = xbc
    xc = _silu(acc)
    xs = xc[:, 0:D_INNER]
    xs_ref[...] = xs
    b_ref[...] = xc[:, D_INNER:D_INNER + N_GROUPS * D_STATE]
    c_ref[...] = xc[:, D_INNER + N_GROUPS * D_STATE:CONV_DIM]
    dt = _softplus(dt_raw + dtb_ref[...])
    da = jnp.exp(dt * (-jnp.exp(alog_ref[...])))
    xdt_ref[...] = xs * _dot_f32_lhs(dt, e_ref[...])
    da_ref[...] = _dot_f32_lhs(da, e_ref[...])


def _ssd_sample_pre(x2d, conv_state_t, p):
    nb = x2d.shape[0]
    full = lambda shape: pl.BlockSpec(shape, lambda i: (0,) * len(shape))
    return pl.pallas_call(
        _ssd_sample_pre_kernel,
        grid=(1,),
        in_specs=[full((nb, D_MODEL)), full((1, D_MODEL)), full((D_MODEL, W_SSD_COLS)),
                  full((CONV_W - 1, nb, CONV_DIM)), full((CONV_W, CONV_DIM)), full((1, CONV_DIM)),
                  full((1, DT_PAD)), full((1, DT_PAD)), full((DT_PAD, D_INNER))],
        out_specs=[full((nb, D_INNER)), full((nb, D_INNER)), full((nb, N_GROUPS * D_STATE)),
                   full((nb, N_GROUPS * D_STATE)), full((nb, D_INNER)), full((nb, D_INNER)),
                   full((CONV_W - 1, nb, CONV_DIM))],
        out_shape=[jax.ShapeDtypeStruct((nb, D_INNER), F32), jax.ShapeDtypeStruct((nb, D_INNER), F32),
                   jax.ShapeDtypeStruct((nb, N_GROUPS * D_STATE), F32),
                   jax.ShapeDtypeStruct((nb, N_GROUPS * D_STATE), F32),
                   jax.ShapeDtypeStruct((nb, D_INNER), F32), jax.ShapeDtypeStruct((nb, D_INNER), F32),
                   jax.ShapeDtypeStruct((CONV_W - 1, nb, CONV_DIM), F32)],
        compiler_params=pltpu.CompilerParams(vmem_limit_bytes=VMEM_LIMIT),
        name="ssd_sample_pre",
    )(x2d, p["g1"], p["w_ssd"], conv_state_t, p["conv_w"], p["conv_b"], p["dtb"], p["alog"], p["expand"])


def _ssd_state_group(i, g, st_ref, xdt_cols, da_cols, b_ref, c_ref, xs_ref, dskip_ref, st_out_ref, y_ref):
    per_group = D_INNER // N_GROUPS // LANES
    c_bf = c_ref[i].astype(BF16)
    for c in range(g * per_group, (g + 1) * per_group):
        rows = slice(c * LANES, (c + 1) * LANES)
        s_new = st_ref[i, rows, :] * da_cols[:, c:c + 1] + xdt_cols[:, c:c + 1] * b_ref[i, g:g + 1, :]
        st_out_ref[i, rows, :] = s_new
        y_t = _dot_nt(c_bf, s_new.astype(BF16))
        y_ref[i, :, rows] = (y_t[g:g + 1, :] + dskip_ref[:, rows] * xs_ref[i, :, rows]).astype(y_ref.dtype)


def _merge_decode_ssd_kernel(x_ref, yp_ref, z_ref, oa_ref, g1_ref, ng_ref, wg_ref, ws_ref, wa_ref, wo_ref,
                             st_ref, xdt_ref, da_ref, b_ref, c_ref, xs_ref, dskip_ref,
                             out_ref, st_out_ref, y_ref, ys_scr, *, bb):
    cols = [(_to_cols(xdt_ref[i]), _to_cols(da_ref[i])) for i in range(bb)]

    def group_thunk(g):
        def run():
            for i in range(bb):
                _ssd_state_group(i, g, st_ref, cols[i][0], cols[i][1], b_ref, c_ref, xs_ref, dskip_ref,
                                 st_out_ref, y_ref)
        return run

    _merge_body(x_ref, yp_ref, z_ref, oa_ref, g1_ref, ng_ref, wg_ref, ws_ref, wa_ref, wo_ref, out_ref, ys_scr,
                [group_thunk(g) for g in range(N_GROUPS)])


def _merge_decode_ssd(x2d, y_pre, z, o_attn, g1, ng, w_gates, w_ssm_out, w_attn_out, w_o,
                      state, xdt, da, bm, cm, xs, dskip_e, *, tm):
    m = x2d.shape[0]
    nb = state.shape[0]
    steps = m // tm
    assert nb % steps == 0
    bb = nb // steps
    in_specs, out_spec = _merge_specs(tm)
    blk3 = lambda shape: pl.BlockSpec((bb,) + shape, lambda i: (i, 0, 0))
    x1, st_new, y_s = pl.pallas_call(
        functools.partial(_merge_decode_ssd_kernel, bb=bb),
        grid=(steps,),
        in_specs=in_specs + [blk3((D_INNER, D_STATE)), blk3((D_INNER // LANES, LANES)),
                             blk3((D_INNER // LANES, LANES)),
                             blk3((N_GROUPS, D_STATE)), blk3((N_GROUPS, D_STATE)), blk3((1, D_INNER)),
                             _resident((1, D_INNER))],
        out_specs=[out_spec, blk3((D_INNER, D_STATE)), blk3((1, D_INNER))],
        out_shape=[jax.ShapeDtypeStruct((m, D_MODEL), F32), jax.ShapeDtypeStruct((nb, D_INNER, D_STATE), F32),
                   jax.ShapeDtypeStruct((nb, 1, D_INNER), BF16)],
        scratch_shapes=[pltpu.VMEM((tm, D_INNER), BF16)],
        compiler_params=pltpu.CompilerParams(
            dimension_semantics=("parallel",), vmem_limit_bytes=VMEM_LIMIT),
        name="merge_decode_ssd",
    )(x2d, y_pre, z, o_attn, g1, ng, w_gates, w_ssm_out, w_attn_out, w_o,
      state, xdt.reshape(nb, D_INNER // LANES, LANES), da.reshape(nb, D_INNER // LANES, LANES),
      bm.reshape(nb, N_GROUPS, D_STATE), cm.reshape(nb, N_GROUPS, D_STATE), xs.reshape(nb, 1, D_INNER), dskip_e)
    return x1, st_new, y_s.reshape(nb, D_INNER)


def _proj_rows_kernel(x_ref, g1_ref, w_ref, out_ref):
    h = _rmsnorm_rows(x_ref[...], g1_ref[...]).astype(BF16)
    out_ref[...] = _dot(h, w_ref[...])


def _proj_rows(x2d, g1, w):
    nb, cols = x2d.shape[0], w.shape[1]
    full = lambda shape: pl.BlockSpec(shape, lambda i: (0,) * len(shape))
    return pl.pallas_call(
        _proj_rows_kernel,
        grid=(1,),
        in_specs=[full((nb, D_MODEL)), full((1, D_MODEL)), full((D_MODEL, cols))],
        out_specs=full((nb, cols)),
        out_shape=jax.ShapeDtypeStruct((nb, cols), F32),
        compiler_params=pltpu.CompilerParams(vmem_limit_bytes=VMEM_LIMIT),
        name="proj_rows",
    )(x2d, g1, w)


def _attn_sample_head(i, h, qkv_cols, cache_refs, o_c_ref):
    kv0 = 3 * ATTN_WIDTH
    hrows = slice(h * ATTN_DIM, (h + 1) * ATTN_DIM)

    def col(first):
        c, o = divmod(first, LANES)
        return qkv_cols[o:o + ATTN_DIM, c:c + 1]
    scores, new_scores = [], []
    for g, c_ref in enumerate(cache_refs):
        w, dil = WINDOWS[g], DILATIONS[g]
        q_col = col(g * ATTN_WIDTH + h * ATTN_DIM) * (ATTN_DIM ** -0.5)
        k_col = col(kv0 + g * KV_WIDTH + h * ATTN_DIM)
        s = jnp.sum(c_ref[i, hrows, :] * q_col, axis=0, keepdims=True)
        pos = lax.broadcasted_iota(jnp.int32, (1, w), 1)
        on_stride = (pos & (dil - 1)) == 0
        s = jnp.where(on_stride, s - _alibi_slope(g, h) * (w - pos).astype(F32), NEG_BIG)
        scores.append(s)
        new_scores.append(jnp.sum(q_col * k_col, axis=0, keepdims=True))
    m = new_scores[0]
    for g in range(3):
        m = jnp.maximum(m, jnp.maximum(new_scores[g], jnp.max(scores[g], axis=-1, keepdims=True)))
    den = jnp.zeros((1, 1), F32)
    acc = jnp.zeros((ATTN_DIM, 1), F32)
    for g, c_ref in enumerate(cache_refs):
        p = jnp.exp(scores[g] - m)
        p_new = jnp.exp(new_scores[g] - m)
        den = den + jnp.sum(p, axis=-1, keepdims=True) + p_new
        v_col = col(kv0 + g * KV_WIDTH + ATTN_WIDTH + h * ATTN_DIM)
        v_t = c_ref[i, ATTN_WIDTH + h * ATTN_DIM:ATTN_WIDTH + (h + 1) * ATTN_DIM, :]
        acc = acc + jnp.sum(v_t * p, axis=-1, keepdims=True) + p_new * v_col
    o_c_ref[i, hrows, :] = acc / den


def _ffn_decode_attn_kernel(x_ref, g2_ref, gf_ref, wg_ref, wu_ref, wd_ref, qkv_ref, c0_ref, c1_ref, c2_ref,
                            out_ref, o_c_ref, act_scr, *, bb):
    heads = []
    for i in range(bb):
        qkv_cols = _to_cols(qkv_ref[i])
        heads += [functools.partial(_attn_sample_head, i, h, qkv_cols, (c0_ref, c1_ref, c2_ref), o_c_ref)
                  for h in range(ATTN_HEADS)]
    _ffn_body(x_ref, g2_ref, gf_ref, wg_ref, wu_ref, wd_ref, out_ref, act_scr, heads)


def _ffn_decode_attn(x2d, g2, gf, w_gate, w_up, w_down, qkv, caches, *, tm):
    m = x2d.shape[0]
    nb = qkv.shape[0]
    steps = m // tm
    assert nb % steps == 0
    bb = nb // steps
    views = []
    for g, c in enumerate(caches):
        assert c.shape[1] == WINDOWS[g]
        views.append(jnp.transpose(c, (0, 2, 3, 4, 1)).reshape(nb, KV_WIDTH, WINDOWS[g]))
    in_specs, out_spec = _ffn_specs(tm)
    y, o_c = pl.pallas_call(
        functools.partial(_ffn_decode_attn_kernel, bb=bb),
        grid=(steps,),
        in_specs=in_specs + [pl.BlockSpec((bb, QKV_COLS // LANES, LANES), lambda i: (i, 0, 0))]
        + [pl.BlockSpec((bb, KV_WIDTH, w), lambda i: (i, 0, 0)) for w in WINDOWS],
        out_specs=[out_spec, pl.BlockSpec((bb, ATTN_WIDTH, 1), lambda i: (i, 0, 0))],
        out_shape=[jax.ShapeDtypeStruct((m, D_MODEL), F32), jax.ShapeDtypeStruct((nb, ATTN_WIDTH, 1), F32)],
        scratch_shapes=[pltpu.VMEM((tm, D_FF), BF16)],
        compiler_params=pltpu.CompilerParams(
            dimension_semantics=("parallel",), vmem_limit_bytes=VMEM_LIMIT),
        name="ffn_decode_attn",
    )(x2d, g2, gf, w_gate, w_up, w_down, qkv.reshape(nb, QKV_COLS // LANES, LANES), *views)
    return y, o_c.reshape(nb, ATTN_WIDTH)


def _prep_attn_weight(w_in):
    cols = [w_in[:, OFF_Q:OFF_Q + 3 * ATTN_WIDTH]]
    for g in range(3):
        cols.append(w_in[:, OFF_K + g * ATTN_WIDTH:OFF_K + (g + 1) * ATTN_WIDTH])
        cols.append(w_in[:, OFF_V + g * ATTN_WIDTH:OFF_V + (g + 1) * ATTN_WIDTH])
    return jnp.concatenate(cols, axis=1).astype(BF16)


def _layers(x, x_s, state_ssm, state_conv, caches, wp):
    n, s, _ = x.shape
    nb = x_s.shape[0]
    p = wp["ssd"]
    ffn_w = (wp["g2"], wp["gf"], wp["w_ffn_gate"], wp["w_ffn_up"], wp["w_ffn_down"])
    merge_w = (p["g1"], wp["ng"], wp["w_gates"], wp["w_ssm_out"], wp["w_attn_out"], wp["w_o"])

    y_pre, z, st, conv = _ssd_prompt(x, **p)
    o_attn, t0, t1, t2 = _attn_prompt(x, p["g1"], wp["w_qkv"])
    z_s, xs, bm, cm, xdt, da, conv_s = _ssd_sample_pre(x_s, jnp.transpose(state_conv, (1, 0, 2)), p)
    qkv_s = _proj_rows(x_s, p["g1"], wp["w_qkv"])

    x1, st_s, y_pre_s = _merge_decode_ssd(
        x.reshape(n * s, D_MODEL), y_pre.reshape(n * s, D_INNER), z.reshape(n * s, D_INNER),
        o_attn.reshape(n * s, ATTN_WIDTH), *merge_w,
        state_ssm.reshape(nb, D_INNER, D_STATE), xdt, da, bm, cm, xs, p["dskip_e"], tm=512)

    y, o_s = _ffn_decode_attn(x1, *ffn_w, qkv_s, caches, tm=512)

    x1_s = _merge(x_s, y_pre_s, z_s, o_s, *merge_w, tm=nb)
    y_s = _ffn(x1_s, *ffn_w, tm=nb)
    prompt = (y.reshape(n, s, D_MODEL), st, conv, (t0, t1, t2))
    sample = (y_s, st_s, conv_s, qkv_s[:, 3 * ATTN_WIDTH:])
    return prompt, sample


def _prep_weights(norm1_g, w_in, conv_w, conv_b, dt_bias, a_log, d_skip, ssm_norm_g, w_ssm_out, w_attn_out,
                  w_o, norm2_g, w_ffn_gate, w_ffn_up, w_ffn_down, norm_f_g):
    return dict(
        ssd=_prep_ssd_params(norm1_g, w_in, conv_w, conv_b, dt_bias, a_log, d_skip),
        w_qkv=_prep_attn_weight(w_in), ng=ssm_norm_g.reshape(1, D_INNER),
        w_gates=w_in[:, OFF_GS:OFF_GS + 2 * D_MODEL].astype(BF16),
        w_ssm_out=w_ssm_out.astype(BF16), w_attn_out=w_attn_out.astype(BF16), w_o=w_o.astype(BF16),
        g2=norm2_g.reshape(1, D_MODEL), gf=norm_f_g.reshape(1, D_MODEL),
        w_ffn_gate=w_ffn_gate.astype(BF16), w_ffn_up=w_ffn_up.astype(BF16),
        w_ffn_down=w_ffn_down.astype(BF16))


def kernel(x_prompt, x_sample, state_ssm, state_conv, cache_kv_w128, cache_kv_w512, cache_kv_w2048, norm1_g, w_in, conv_w, conv_b, dt_bias, a_log, d_skip, ssm_norm_g, w_ssm_out, w_attn_out, w_o, norm2_g, w_ffn_gate, w_ffn_up, w_ffn_down, norm_f_g):
    wp = _prep_weights(norm1_g[0], w_in[0], conv_w[0], conv_b[0], dt_bias[0], a_log[0], d_skip[0],
                       ssm_norm_g[0], w_ssm_out[0], w_attn_out[0], w_o[0], norm2_g[0], w_ffn_gate[0],
                       w_ffn_up[0], w_ffn_down[0], norm_f_g)
    n, s, _ = x_prompt.shape
    nb = x_sample.shape[0]
    (y_p, st_p, conv_p, tails), (y_s, st_s, conv_s, kv_s) = _layers(
        x_prompt, x_sample.reshape(nb, D_MODEL), state_ssm[0], state_conv[0],
        (cache_kv_w128[0], cache_kv_w512[0], cache_kv_w2048[0]), wp)
    kv_tail_shape = (2, ATTN_HEADS, ATTN_DIM)
    outs = [y_p, y_s.reshape(nb, 1, D_MODEL),
            st_p.reshape(1, n, N_HEADS, HEAD_DIM, D_STATE), st_s.reshape(1, nb, N_HEADS, HEAD_DIM, D_STATE),
            conv_p.reshape(1, n, CONV_W - 1, CONV_DIM), jnp.transpose(conv_s, (1, 0, 2)).reshape(1, nb, CONV_W - 1, CONV_DIM)]
    for g in range(3):
        outs.append(tails[g].reshape((1, n, WINDOWS[g]) + kv_tail_shape))
        outs.append(kv_s[:, g * KV_WIDTH:(g + 1) * KV_WIDTH].reshape((1, nb, 1) + kv_tail_shape))
    return tuple(outs)
```
